```python
import math, functools
import jax, jax.numpy as jnp
from jax import lax
import numpy as np

D_MODEL = 2048
BATCH = 4
SEQ = 2048
DEPTH = 1
DEC_BATCH = 32
DEC_SEQ = 1
PAST_LEN = 16384
PAGE_SIZE = 128

ATT_HEADS = 8
ATT_DK = 64
ATT_DV = 2 * ATT_DK
ATT_QK_W = ATT_HEADS * 2 * ATT_DK
ATT_W = ATT_HEADS * ATT_DV
ATT_SCALE = ATT_DK ** -0.5
Q_BLOCK = 128
N_BUCKETS = 32
MAX_DISTANCE = 128
RW_HEADS = 16
RW_N = 64
RW_W = RW_HEADS * RW_N
RW_DECAY_LORA = 64
RW_AAA_LORA = 64
RW_GATE_LORA = 128
RW_PROJ = 3 * RW_W + RW_DECAY_LORA + RW_AAA_LORA + RW_GATE_LORA
IN_W = 2 * ATT_QK_W + ATT_W + RW_PROJ + 2 * D_MODEL
N_GROUPS = 8
EXPERTS_PER_GROUP = 8
N_EXPERTS = N_GROUPS * EXPERTS_PER_GROUP
TOPK_IN_GROUP = 2
D_FF_EXPERT = 512
MOE_BLOCK = 128
MOE_MIN_BLOCK = 8
NORM_EPS = 1e-6
SUBLN_EPS = 1e-5
RW_LN_EPS = 64e-5

kernel_name = "diffattn_rwkv7_hmoe_decode_step"


def rms_norm(x, w, eps=NORM_EPS):
    xf = x.astype(jnp.float32)
    y = xf * lax.rsqrt(jnp.mean(xf * xf, axis=-1, keepdims=True) + eps)
    return (y * w.astype(jnp.float32)).astype(x.dtype)


def rel_bucket(dist):
    n = jnp.maximum(dist, 0)
    max_exact = N_BUCKETS // 2
    large = max_exact + (jnp.log(jnp.maximum(n, 1).astype(jnp.float32) / max_exact)
                         / math.log(MAX_DISTANCE / max_exact) * (N_BUCKETS - max_exact)).astype(jnp.int32)
    large = jnp.minimum(large, N_BUCKETS - 1)
    return jnp.where(n < max_exact, n, large)


def rel_bias_heads(rel_bias, dist):
    return jnp.transpose(rel_bias[rel_bucket(dist)], (2, 0, 1)).astype(jnp.float32)


def diff_combine(s, lam, dtype):
    p = jax.nn.softmax(s, axis=-1)
    return (p[:, :, 0] - lam * p[:, :, 1]).astype(dtype)


def diff_attn_prompt(q, k, v, lam, rel_bias):
    B, S = q.shape[0], q.shape[1]
    n_blk = S // Q_BLOCK
    k_pos = jnp.arange(S)
    q_blocks = jnp.moveaxis(q.reshape(B, n_blk, Q_BLOCK, ATT_HEADS, 2, ATT_DK), 1, 0)

    def one_block(args):
        q_blk, i = args
        q_pos = i * Q_BLOCK + jnp.arange(Q_BLOCK)
        dist = q_pos[:, None] - k_pos[None, :]
        s = jnp.einsum("bqhcd,bkhcd->bhcqk", q_blk, k).astype(jnp.float32) * ATT_SCALE
        s = s + rel_bias_heads(rel_bias, dist)[None, :, None]
        s = jnp.where(dist >= 0, s, -jnp.inf)
        a = diff_combine(s, lam, v.dtype)
        return jnp.einsum("bhqk,bkhd->bqhd", a, v)

    o = lax.map(one_block, (q_blocks, jnp.arange(n_blk)))
    return jnp.moveaxis(o, 0, 1).reshape(B, S, ATT_HEADS, ATT_DV)


def diff_attn_sample(q, k, v, lam, k_past, v_past, rel_bias):
    T = q.shape[1]
    P = k_past.shape[1]
    q_pos = P + jnp.arange(T)
    dist_past = q_pos[:, None] - jnp.arange(P)[None, :]
    dist_new = q_pos[:, None] - q_pos[None, :]
    s_past = jnp.einsum("bqhcd,bkhcd->bhcqk", q, k_past).astype(jnp.float32) * ATT_SCALE
    s_past = s_past + rel_bias_heads(rel_bias, dist_past)[None, :, None]
    s_new = jnp.einsum("bqhcd,bkhcd->bhcqk", q, k).astype(jnp.float32) * ATT_SCALE
    s_new = s_new + rel_bias_heads(rel_bias, dist_new)[None, :, None]
    s_new = jnp.where(dist_new >= 0, s_new, -jnp.inf)
    a = diff_combine(jnp.concatenate([s_past, s_new], axis=-1), lam, v.dtype)
    return (jnp.einsum("bhqk,bkhd->bqhd", a[..., :P], v_past)
            + jnp.einsum("bhqk,bkhd->bqhd", a[..., P:], v))


def rwkv_step(S, inp):
    r, w, k, v, a_vec, b_vec = inp
    Sa = jnp.einsum("bhvk,bhk->bhv", S, a_vec)
    S = S * w[:, :, None, :] + Sa[..., None] * b_vec[:, :, None, :] + v[..., None] * k[:, :, None, :]
    y = jnp.einsum("bhvk,bhk->bhv", S, r)
    return S, y


def rwkv_time_mix(p_mix, S0, lp):
    B, T = p_mix.shape[0], p_mix.shape[1]
    pf = p_mix.astype(jnp.float32)
    r, kx, vx, wl, al, gl = jnp.split(
        pf, [RW_W, 2 * RW_W, 3 * RW_W, 3 * RW_W + RW_DECAY_LORA, 3 * RW_W + RW_DECAY_LORA + RW_AAA_LORA], axis=-1)
    w = -jax.nn.softplus(-(lp["rw_w0"] + jnp.tanh(wl) @ lp["rw_w2"])) - 0.5
    decay = jnp.exp(-jnp.exp(w))
    a = jax.nn.sigmoid(lp["rw_a0"] + al @ lp["rw_a2"])
    g = jax.nn.sigmoid(gl) @ lp["rw_g2"]
    heads = lambda t: t.reshape(B, T, RW_HEADS, RW_N)
    kk = heads(kx * lp["rw_k_k"])
    kk = kk / jnp.maximum(jnp.linalg.norm(kk, axis=-1, keepdims=True), 1e-12)
    k = kx * (1.0 + (a - 1.0) * lp["rw_k_a"])
    rh, kh, vh, ah, dh = heads(r), heads(k), heads(vx), heads(a), heads(decay)
    xs = tuple(jnp.moveaxis(t.astype(jnp.float32), 1, 0) for t in (rh, dh, kh, vh, -kk, kk * ah))
    S_fin, ys = lax.scan(rwkv_step, S0.astype(jnp.float32), xs)
    y = jnp.moveaxis(ys, 0, 1)
    mu = jnp.mean(y, axis=-1, keepdims=True)
    var = jnp.mean(jnp.square(y - mu), axis=-1, keepdims=True)
    y = (y - mu) * lax.rsqrt(var + RW_LN_EPS)
    y = y.reshape(B, T, RW_W) * lp["rw_ln_w"] + lp["rw_ln_b"]
    bonus = jnp.sum(rh * kh * lp["rw_r_k"], axis=-1, keepdims=True) * vh
    y = (y + bonus.reshape(B, T, RW_W)) * g
    return y.astype(p_mix.dtype), S_fin.astype(S0.dtype)


def token_mixer(xn, lp, lam_init, attend, shift_prev, wkv0):
    B, T = xn.shape[0], xn.shape[1]
    p = xn @ lp["w_in"]
    q, k, v, p_rw, g_lin = jnp.split(
        p, [ATT_QK_W, 2 * ATT_QK_W, 2 * ATT_QK_W + ATT_W, 2 * ATT_QK_W + ATT_W + RW_PROJ], axis=-1)
    q = q.reshape(B, T, ATT_HEADS, 2, ATT_DK)
    k = k.reshape(B, T, ATT_HEADS, 2, ATT_DK)
    v = v.reshape(B, T, ATT_HEADS, ATT_DV)
    f32 = jnp.float32
    lam = (jnp.exp(jnp.sum(lp["lambda_q1"].astype(f32) * lp["lambda_k1"].astype(f32)))
           - jnp.exp(jnp.sum(lp["lambda_q2"].astype(f32) * lp["lambda_k2"].astype(f32))) + lam_init)
    o = attend(q, k, v, lam)
    o = rms_norm(o, lp["subln_w"], SUBLN_EPS) * (1.0 - lam_init)
    y_att = o.reshape(B, T, ATT_W)
    p_prev = jnp.concatenate([shift_prev[:, None].astype(p_rw.dtype), p_rw[:, :-1]], axis=1)
    p_mix = p_rw + lp["rw_mu"] * (p_prev - p_rw)
    y_rw, wkv_new = rwkv_time_mix(p_mix, wkv0, lp)
    gates = jax.nn.sigmoid(g_lin + lp["b_gate"])
    g_a, g_b = jnp.split(gates, 2, axis=-1)
    merged = g_a * (y_att @ lp["w_branch_a"]) + g_b * (y_rw @ lp["w_branch_b"])
    out = merged @ lp["w_out"]
    return out, k.reshape(B, T, ATT_HEADS, 2 * ATT_DK), v, wkv_new, p_rw[:, -1]


def moe_ffn(x2, lp):
    n_tok = x2.shape[0]
    f32 = jnp.float32
    lg = (x2 @ lp["w_router_g"]).astype(f32) + lp["b_router_g"].astype(f32)
    pg = jax.nn.softmax(lg, axis=-1)
    g_idx = jnp.argmax(lg, axis=-1).astype(jnp.int32)
    g_w = jnp.take_along_axis(pg, g_idx[:, None], axis=-1)
    le = ((x2 @ lp["w_router_e"]).astype(f32) + lp["b_router_e"].astype(f32)).reshape(
        n_tok, N_GROUPS, EXPERTS_PER_GROUP)
    le_g = jnp.take_along_axis(le, g_idx[:, None, None], axis=1)[:, 0]
    pe = jax.nn.softmax(le_g, axis=-1)
    top_p, top_i = lax.top_k(pe, TOPK_IN_GROUP)
    weights = g_w * top_p / jnp.sum(top_p, axis=-1, keepdims=True)
    expert_id = g_idx[:, None] * EXPERTS_PER_GROUP + top_i.astype(jnp.int32)
    n_rows = n_tok * TOPK_IN_GROUP
    blk = MOE_BLOCK
    while blk > MOE_MIN_BLOCK and blk * N_EXPERTS > n_rows:
        blk //= 2
    n_blocks = -(-n_rows // blk) + N_EXPERTS
    flat_e = expert_id.reshape(n_rows)
    flat_tok = jnp.repeat(jnp.arange(n_tok, dtype=jnp.int32), TOPK_IN_GROUP)
    flat_w = weights.reshape(n_rows)
    order = jnp.argsort(flat_e)
    sorted_e = flat_e[order]
    counts = jax.ops.segment_sum(jnp.ones((n_rows,), jnp.int32), flat_e, num_segments=N_EXPERTS)
    padded = (counts + blk - 1) // blk * blk
    p_end = jnp.cumsum(padded)
    dest = ((p_end - padded)[sorted_e] + jnp.arange(n_rows, dtype=jnp.int32)
            - (jnp.cumsum(counts) - counts)[sorted_e])
    row_tok = jnp.zeros((n_blocks * blk,), jnp.int32).at[dest].set(flat_tok[order])
    row_w = jnp.zeros((n_blocks * blk,), f32).at[dest].set(flat_w[order])
    block_e = jnp.minimum(
        jnp.searchsorted(p_end, jnp.arange(n_blocks, dtype=jnp.int32) * blk, side="right"), N_EXPERTS - 1)
    x_rows = x2[row_tok].reshape(n_blocks, blk, x2.shape[1])
    w_gate, w_up, w_down = lp["w_gate_e"], lp["w_up_e"], lp["w_down_e"]

    def expert_block(args):
        xb, e = args
        hdn = jax.nn.silu(xb @ w_gate[e]) * (xb @ w_up[e])
        return hdn @ w_down[e]

    y_rows = lax.map(expert_block, (x_rows, block_e)).reshape(n_blocks * blk, x2.shape[1])
    y = jax.ops.segment_sum(y_rows.astype(f32) * row_w[:, None], row_tok, num_segments=n_tok)
    return y.astype(x2.dtype)


def trunk_layer(x, lp, lam_init, attend, shift_prev, wkv0):
    mix, k_rows, v_rows, wkv_new, shift_new = token_mixer(
        rms_norm(x, lp["norm_mix_w"]), lp, lam_init, attend, shift_prev, wkv0)
    h = x + mix
    B, T, D = h.shape
    f = moe_ffn(rms_norm(h, lp["norm_ffn_w"]).reshape(B * T, D), lp).reshape(B, T, D)
    return h + f, k_rows, v_rows, wkv_new, shift_new


def setup_inputs(seed: int = 0) -> dict:
    key = jax.random.key(seed)
    ks = iter(jax.random.split(key, 48))
    f32 = jnp.float32

    def nrm(shape, scale):
        return scale * jax.random.normal(next(ks), shape, f32)

    def gain(shape):
        return 1.0 + nrm(shape, 0.02)

    L = DEPTH
    n_pages = PAST_LEN // PAGE_SIZE
    n_used = DEC_BATCH * n_pages
    n_pool = n_used + n_used // 4
    x_prompt = nrm((BATCH, SEQ, D_MODEL), 1.0)
    x_sample = nrm((DEC_BATCH, DEC_SEQ, D_MODEL), 1.0)
    cache_k = nrm((L, n_pool, PAGE_SIZE, ATT_HEADS, 2 * ATT_DK), 1.0)
    cache_v = nrm((L, n_pool, PAGE_SIZE, ATT_HEADS, ATT_DV), 1.0)
    state_wkv = nrm((L, DEC_BATCH, RW_HEADS, RW_N, RW_N), 0.1)
    state_shift = nrm((L, DEC_BATCH, RW_PROJ), 1.0)
    page_table = jax.random.permutation(next(ks), n_pool)[:n_used].reshape(DEC_BATCH, n_pages).astype(jnp.int32)
    return {
        "x_prompt": x_prompt,
        "x_sample": x_sample,
        "cache_k": cache_k,
        "cache_v": cache_v,
        "state_wkv": state_wkv,
        "state_shift": state_shift,
        "page_table": page_table,
        "norm_mix_w": gain((L, D_MODEL)),
        "w_in": nrm((L, D_MODEL, IN_W), D_MODEL ** -0.5),
        "b_gate": nrm((L, 2 * D_MODEL), 0.1),
        "lambda_q1": nrm((L, ATT_DK), 0.1),
        "lambda_k1": nrm((L, ATT_DK), 0.1),
        "lambda_q2": nrm((L, ATT_DK), 0.1),
        "lambda_k2": nrm((L, ATT_DK), 0.1),
        "subln_w": gain((L, ATT_DV)),
        "rel_bias": nrm((N_BUCKETS, ATT_HEADS), 0.5),
        "rw_mu": jax.random.uniform(next(ks), (L, RW_PROJ), f32, 0.0, 1.0),
        "rw_w0": jax.random.uniform(next(ks), (L, RW_W), f32, -6.0, -1.0),
        "rw_w2": nrm((L, RW_DECAY_LORA, RW_W), 0.1 * RW_DECAY_LORA ** -0.5),
        "rw_a0": nrm((L, RW_W), 0.1),
        "rw_a2": nrm((L, RW_AAA_LORA, RW_W), 0.5 * RW_AAA_LORA ** -0.5),
        "rw_g2": nrm((L, RW_GATE_LORA, RW_W), RW_GATE_LORA ** -0.5),
        "rw_k_k": 0.85 + nrm((L, RW_W), 0.02),
        "rw_k_a": gain((L, RW_W)),
        "rw_r_k": nrm((L, RW_HEADS, RW_N), 0.1),
        "rw_ln_w": gain((L, RW_W)),
        "rw_ln_b": nrm((L, RW_W), 0.02),
        "w_branch_a": nrm((L, ATT_W, D_MODEL), ATT_W ** -0.5),
        "w_branch_b": nrm((L, RW_W, D_MODEL), RW_W ** -0.5),
        "w_out": nrm((L, D_MODEL, D_MODEL), D_MODEL ** -0.5),
        "norm_ffn_w": gain((L, D_MODEL)),
        "w_router_g": nrm((L, D_MODEL, N_GROUPS), D_MODEL ** -0.5),
        "b_router_g": nrm((L, N_GROUPS), 0.01),
        "w_router_e": nrm((L, D_MODEL, N_EXPERTS), D_MODEL ** -0.5),
        "b_router_e": nrm((L, N_EXPERTS), 0.01),
        "w_gate_e": nrm((L, N_EXPERTS, D_MODEL, D_FF_EXPERT), D_MODEL ** -0.5),
        "w_up_e": nrm((L, N_EXPERTS, D_MODEL, D_FF_EXPERT), D_MODEL ** -0.5),
        "w_down_e": nrm((L, N_EXPERTS, D_FF_EXPERT, D_MODEL), D_FF_EXPERT ** -0.5),
        "norm_final_w": gain((D_MODEL,)),
    }


def reference(x_prompt, x_sample, cache_k, cache_v, state_wkv, state_shift, page_table,
              norm_mix_w, w_in, b_gate, lambda_q1, lambda_k1, lambda_q2, lambda_k2, subln_w, rel_bias,
              rw_mu, rw_w0, rw_w2, rw_a0, rw_a2, rw_g2, rw_k_k, rw_k_a, rw_r_k, rw_ln_w, rw_ln_b,
              w_branch_a, w_branch_b, w_out,
              norm_ffn_w, w_router_g, b_router_g, w_router_e, b_router_e, w_gate_e, w_up_e, w_down_e,
              norm_final_w):
    n_sample = page_table.shape[0]
    past_len = page_table.shape[1] * PAGE_SIZE
    n_prompt = x_prompt.shape[0]
    h_p, h_s = x_prompt, x_sample
    kp_l, vp_l, wp_l, sp_l, ks_l, vs_l, ws_l, ss_l = [], [], [], [], [], [], [], []
    for l in range(DEPTH):
        lp = {
            "norm_mix_w": norm_mix_w[l], "w_in": w_in[l], "b_gate": b_gate[l],
            "lambda_q1": lambda_q1[l], "lambda_k1": lambda_k1[l],
            "lambda_q2": lambda_q2[l], "lambda_k2": lambda_k2[l], "subln_w": subln_w[l],
            "rw_mu": rw_mu[l], "rw_w0": rw_w0[l], "rw_w2": rw_w2[l], "rw_a0": rw_a0[l], "rw_a2": rw_a2[l],
            "rw_g2": rw_g2[l], "rw_k_k": rw_k_k[l], "rw_k_a": rw_k_a[l], "rw_r_k": rw_r_k[l],
            "rw_ln_w": rw_ln_w[l], "rw_ln_b": rw_ln_b[l],
            "w_branch_a": w_branch_a[l], "w_branch_b": w_branch_b[l], "w_out": w_out[l],
            "norm_ffn_w": norm_ffn_w[l], "w_router_g": w_router_g[l], "b_router_g": b_router_g[l],
            "w_router_e": w_router_e[l], "b_router_e": b_router_e[l],
            "w_gate_e": w_gate_e[l], "w_up_e": w_up_e[l], "w_down_e": w_down_e[l],
        }
        lam_init = 0.8 - 0.6 * math.exp(-0.3 * l)
        attend_p = functools.partial(diff_attn_prompt, rel_bias=rel_bias)
        shift0 = jnp.zeros((n_prompt, RW_PROJ), x_prompt.dtype)
        wkv0 = jnp.zeros((n_prompt, RW_HEADS, RW_N, RW_N), state_wkv.dtype)
        h_p, kp, vp, wp, sp = trunk_layer(h_p, lp, lam_init, attend_p, shift0, wkv0)
        k_past = cache_k[l, page_table].reshape(n_sample, past_len, ATT_HEADS, 2, ATT_DK)
        v_past = cache_v[l, page_table].reshape(n_sample, past_len, ATT_HEADS, ATT_DV)
        attend_s = functools.partial(diff_attn_sample, k_past=k_past, v_past=v_past, rel_bias=rel_bias)
        h_s, ks_, vs_, ws_, ss_ = trunk_layer(h_s, lp, lam_init, attend_s, state_shift[l], state_wkv[l])
        kp_l.append(kp); vp_l.append(vp); wp_l.append(wp); sp_l.append(sp)
        ks_l.append(ks_); vs_l.append(vs_); ws_l.append(ws_); ss_l.append(ss_)
    y_prompt = rms_norm(h_p, norm_final_w)
    y_sample = rms_norm(h_s, norm_final_w)
    return (y_prompt, y_sample,
            jnp.stack(kp_l), jnp.stack(vp_l), jnp.stack(wp_l), jnp.stack(sp_l),
            jnp.stack(ks_l), jnp.stack(vs_l), jnp.stack(ws_l), jnp.stack(ss_l))
```

```python
import functools
import math

import jax
import jax.numpy as jnp
from jax import lax
from jax.experimental import pallas as pl
from jax.experimental.pallas import tpu as pltpu

F32 = jnp.float32
BF16 = jnp.bfloat16
I32 = jnp.int32

D_MODEL = 2048
ATT_HEADS = 8
ATT_DK = 64
HEAD_W = 2 * ATT_DK
ATT_W = ATT_HEADS * HEAD_W
ATT_SCALE = ATT_DK ** -0.5
RW_HEADS = 16
RW_N = 64
RW_W = RW_HEADS * RW_N
RW_LORA_W = 128
RW_GATE_W = 128
RW_PROJ = 3 * RW_W + RW_LORA_W + RW_GATE_W
N_BUCKETS = 32
MAX_DISTANCE = 128
PAGE = 128
EXPERTS_PER_GROUP = 8
N_GROUPS = 8
N_EXPERTS = N_GROUPS * EXPERTS_PER_GROUP
D_FF = 512
NORM_EPS = 1e-6
SUBLN_EPS = 1e-5
RW_LN_EPS = 64e-5
LAM_INIT = 0.8 - 0.6 * math.exp(-0.3 * 0)
NEG = -1e30
LANES = 128
MOE_BLK = 128
ROW_TILE = 256
MIB = 1024 * 1024


def _cparams(sem, vmem_mib):
    return pltpu.CompilerParams(dimension_semantics=sem, vmem_limit_bytes=vmem_mib * MIB)


def _full(shape):
    nd = len(shape)
    return pl.BlockSpec(shape, lambda *_: (0,) * nd)


def _rmsnorm_bf16_kernel(x_ref, w_ref, o_ref):
    x = x_ref[...]
    ms = jnp.mean(x * x, axis=-1, keepdims=True)
    o_ref[...] = (x * lax.rsqrt(ms + NORM_EPS) * w_ref[...]).astype(BF16)


def rmsnorm_bf16(x, w):
    m, d = x.shape
    tm = min(m, 512)
    return pl.pallas_call(
        _rmsnorm_bf16_kernel,
        grid=(m // tm,),
        in_specs=[pl.BlockSpec((tm, d), lambda i: (i, 0)), _full((1, d))],
        out_specs=pl.BlockSpec((tm, d), lambda i: (i, 0)),
        out_shape=jax.ShapeDtypeStruct((m, d), BF16),
        compiler_params=_cparams(("parallel",), 32),
        name="rmsnorm_bf16",
    )(x, w.reshape(1, d))


def _matmul_kernel(x_ref, w_ref, o_ref):
    o_ref[...] = jnp.dot(x_ref[...], w_ref[...].astype(BF16),
                         preferred_element_type=F32).astype(o_ref.dtype)


def matmul_cols(x, w, col0, ncols, out_dtype):
    m, k = x.shape
    tm = next(t for t in (2048, 1024, 512, 256, m) if m % t == 0)
    tn = next(t for t in (512, 256, 128) if col0 % t == 0 and ncols % t == 0)
    off = col0 // tn
    return pl.pallas_call(
        _matmul_kernel,
        grid=(m // tm, ncols // tn),
        in_specs=[pl.BlockSpec((tm, k), lambda i, j: (i, 0)),
                  pl.BlockSpec((k, tn), lambda i, j: (0, j + off))],
        out_specs=pl.BlockSpec((tm, tn), lambda i, j: (i, j)),
        out_shape=jax.ShapeDtypeStruct((m, ncols), out_dtype),
        compiler_params=_cparams(("parallel", "arbitrary"), 48),
        name="in_proj",
    )(x, w)


def _rel_bucket(dist):
    n = jnp.maximum(dist, 0)
    max_exact = N_BUCKETS // 2
    large = max_exact + (jnp.log(jnp.maximum(n, 1).astype(F32) / max_exact)
                         / math.log(MAX_DISTANCE / max_exact) * (N_BUCKETS - max_exact)).astype(I32)
    large = jnp.minimum(large, N_BUCKETS - 1)
    return jnp.where(n < max_exact, n, large)


def _bias_by_distance(rel_bias, n):
    d = jnp.arange(n, dtype=I32)
    table = rel_bias.astype(F32)[_rel_bucket(d)]
    far = rel_bias.astype(F32)[N_BUCKETS - 1]
    return (table - far[None, :]).T


def _lambda(lamv_ref):
    lv = lamv_ref[...]
    s1 = jnp.sum(lv[0:1] * lv[1:2], axis=-1, keepdims=True)
    s2 = jnp.sum(lv[2:3] * lv[3:4], axis=-1, keepdims=True)
    return jnp.exp(s1) - jnp.exp(s2) + LAM_INIT


def _split_halves(q):
    lane = lax.broadcasted_iota(I32, q.shape, 1)
    return jnp.concatenate([jnp.where(lane < ATT_DK, q, 0.0), jnp.where(lane >= ATT_DK, q, 0.0)], axis=0)


def _attn_prompt_kernel(lamv_ref, q_ref, k_ref, v_ref, bd_ref, bp_ref, sw_ref, o_ref, kb_ref, vb_ref, *, t, nq):
    kb_ref[...] = k_ref[...].astype(BF16)
    vb_ref[...] = v_ref[...].astype(BF16)
    lam = _lambda(lamv_ref)
    bd = bd_ref[0]
    bp = bp_ref[0]
    bd2 = jnp.concatenate([bd, bd], axis=0)
    bp2 = jnp.concatenate([bp, bp], axis=0)
    sw = sw_ref[...] * (1.0 - LAM_INIT)

    def block(q2, j0, bias, carry):
        m, l, acc = carry
        kj = kb_ref[pl.ds(j0, t), :]
        s = lax.dot_general(q2, kj, (((1,), (1,)), ((), ())), preferred_element_type=F32)
        if bias is not None:
            s = s + bias
        mn = jnp.maximum(m, jnp.max(s, axis=-1, keepdims=True))
        alpha = jnp.exp(m - mn)
        e = jnp.exp(s - mn)
        l = alpha * l + jnp.sum(e, axis=-1, keepdims=True)
        acc = alpha * acc + jnp.dot(e.astype(BF16), vb_ref[pl.ds(j0, t), :], preferred_element_type=F32)
        return mn, l, acc

    for i in range(nq):
        q = q_ref[i * t:(i + 1) * t, :].astype(F32) * ATT_SCALE
        q2 = _split_halves(q).astype(BF16)
        carry = (jnp.full((2 * t, 1), NEG, F32), jnp.zeros((2 * t, 1), F32), jnp.zeros((2 * t, HEAD_W), F32))
        if i >= 2:
            carry = lax.fori_loop(
                0, i - 1, lambda j, c: block(q2, pl.multiple_of(j * t, t), None, c), carry)
        if i >= 1:
            carry = block(q2, (i - 1) * t, bp2, carry)
        m, l, acc = block(q2, i * t, bd2, carry)
        o = acc[:t] / l[:t] - lam * (acc[t:] / l[t:])
        ms = jnp.mean(o * o, axis=-1, keepdims=True)
        o_ref[i * t:(i + 1) * t, :] = o * lax.rsqrt(ms + SUBLN_EPS) * sw


def attn_prompt(q, k, v, lamv, rel_bias, subln_w, batch, seq):
    t = min(seq, 256)
    nq = seq // t
    bd0 = _bias_by_distance(rel_bias, 2 * t)
    ii = jnp.arange(t, dtype=I32)
    dist = ii[:, None] - ii[None, :]
    bdiag = jnp.where(dist[None] >= 0, bd0[:, jnp.maximum(dist, 0)], NEG)
    bprev = bd0[:, t + dist]
    blk = lambda: pl.BlockSpec((seq, HEAD_W), lambda b, h: (b, h))
    return pl.pallas_call(
        functools.partial(_attn_prompt_kernel, t=t, nq=nq),
        grid=(batch, ATT_HEADS),
        in_specs=[_full((4, ATT_DK)), blk(), blk(), blk(),
                  pl.BlockSpec((1, t, t), lambda b, h: (h, 0, 0)),
                  pl.BlockSpec((1, t, t), lambda b, h: (h, 0, 0)),
                  _full((1, HEAD_W))],
        out_specs=blk(),
        out_shape=jax.ShapeDtypeStruct((batch * seq, ATT_W), F32),
        scratch_shapes=[pltpu.VMEM((seq, HEAD_W), BF16), pltpu.VMEM((seq, HEAD_W), BF16)],
        compiler_params=_cparams(("parallel", "parallel"), 48),
        name="attn_prompt",
    )(lamv, q, k, v, bdiag, bprev, subln_w.reshape(1, HEAD_W))


def _attn_decode_kernel(pt_ref, lamv_ref, q_ref, kn_ref, vn_ref, bias_ref, bnew_ref, sw_ref, *rest, pages):
    del pt_ref
    k_refs = rest[:pages]
    v_refs = rest[pages:2 * pages]
    o_ref = rest[2 * pages]
    m_ref, l_ref, acc_ref = rest[2 * pages + 1:]
    c = pl.program_id(1)
    last = pl.num_programs(1) - 1

    @pl.when(c == 0)
    def _():
        m_ref[...] = jnp.full(m_ref.shape, NEG, F32)
        l_ref[...] = jnp.zeros(l_ref.shape, F32)
        acc_ref[...] = jnp.zeros(acc_ref.shape, F32)

    q16 = _split_halves(q_ref[0] * ATT_SCALE)
    row_head = lax.broadcasted_iota(I32, (2 * ATT_HEADS, 1), 0) & (ATT_HEADS - 1)
    qh = [jnp.where(row_head == h, q16, 0.0).astype(BF16) for h in range(ATT_HEADS)]

    def head_rows(ref, h):
        return ref[0, pl.ds(h, PAGE, stride=ATT_HEADS), :]

    s_parts = []
    for pp in range(pages // 2):
        sp = None
        for h in range(ATT_HEADS):
            k2 = jnp.concatenate([head_rows(k_refs[2 * pp], h), head_rows(k_refs[2 * pp + 1], h)],
                                 axis=0).astype(BF16)
            d = lax.dot_general(qh[h], k2, (((1,), (1,)), ((), ())), preferred_element_type=F32)
            sp = d if sp is None else sp + d
        s_parts.append(sp)
    s = jnp.concatenate(s_parts, axis=1)
    s = s + bias_ref[...] * (c == last).astype(F32)
    m = m_ref[...]
    mn = jnp.maximum(m, jnp.max(s, axis=-1, keepdims=True))
    alpha = jnp.exp(m - mn)
    e = jnp.exp(s - mn)
    l_ref[...] = alpha * l_ref[...] + jnp.sum(e, axis=-1, keepdims=True)
    m_ref[...] = mn
    eb = e.astype(BF16)
    pv = jnp.zeros((2 * ATT_HEADS, HEAD_W), F32)
    for pp in range(pages // 2):
        ep = eb[:, pp * 2 * PAGE:(pp + 1) * 2 * PAGE]
        for h in range(ATT_HEADS):
            v2 = jnp.concatenate([head_rows(v_refs[2 * pp], h), head_rows(v_refs[2 * pp + 1], h)],
                                 axis=0).astype(BF16)
            r = jnp.dot(ep, v2, preferred_element_type=F32)
            pv = pv + jnp.where(row_head == h, r, 0.0)
    acc_ref[...] = alpha * acc_ref[...] + pv

    @pl.when(c == last)
    def _():
        lam = _lambda(lamv_ref)
        m0 = m_ref[...]
        k16 = jnp.concatenate([kn_ref[0], kn_ref[0]], axis=0)
        v16 = jnp.concatenate([vn_ref[0], vn_ref[0]], axis=0)
        s_new = jnp.sum(q16 * k16, axis=-1, keepdims=True) + bnew_ref[...]
        m1 = jnp.maximum(m0, s_new)
        a1 = jnp.exp(m0 - m1)
        e_new = jnp.exp(s_new - m1)
        l1 = a1 * l_ref[...] + e_new
        acc1 = a1 * acc_ref[...] + e_new * v16
        o = acc1[:ATT_HEADS] / l1[:ATT_HEADS] - lam * (acc1[ATT_HEADS:] / l1[ATT_HEADS:])
        ms = jnp.mean(o * o, axis=-1, keepdims=True)
        o_ref[0] = o * lax.rsqrt(ms + SUBLN_EPS) * (sw_ref[...] * (1.0 - LAM_INIT))


def attn_decode(q, k_new, v_new, cache_k, cache_v, page_table, lamv, rel_bias, subln_w):
    nb, n_pages = page_table.shape
    pages = next(g for g in (8, 4, 2) if n_pages % g == 0)
    n_chunks = n_pages // pages
    width = pages * PAGE
    bd0 = _bias_by_distance(rel_bias, width + 1)
    near = bd0[:, width - jnp.arange(width, dtype=I32)]
    bias_last = jnp.concatenate([near, near], axis=0)
    bnew = jnp.concatenate([bd0[:, 0:1], bd0[:, 0:1]], axis=0)
    row = lambda: pl.BlockSpec((1, ATT_HEADS, HEAD_W), lambda b, c, pt: (b, 0, 0))
    page = lambda g: pl.BlockSpec((1, PAGE * ATT_HEADS, HEAD_W), lambda b, c, pt: (pt[b, c * pages + g], 0, 0))
    const = lambda shape: pl.BlockSpec(shape, lambda b, c, pt: (0,) * len(shape))
    grid_spec = pltpu.PrefetchScalarGridSpec(
        num_scalar_prefetch=1,
        grid=(nb, n_chunks),
        in_specs=[const((4, ATT_DK)), row(), row(), row(), const((2 * ATT_HEADS, width)),
                  const((2 * ATT_HEADS, 1)), const((1, HEAD_W))]
                 + [page(g) for g in range(pages)] + [page(g) for g in range(pages)],
        out_specs=row(),
        scratch_shapes=[pltpu.VMEM((2 * ATT_HEADS, 1), F32), pltpu.VMEM((2 * ATT_HEADS, 1), F32),
                        pltpu.VMEM((2 * ATT_HEADS, HEAD_W), F32)],
    )
    return pl.pallas_call(
        functools.partial(_attn_decode_kernel, pages=pages),
        grid_spec=grid_spec,
        out_shape=jax.ShapeDtypeStruct((nb, ATT_HEADS, HEAD_W), F32),
        compiler_params=_cparams(("parallel", "arbitrary"), 48),
        name="attn_decode",
    )(page_table, lamv, q, k_new, v_new, bias_last, bnew, subln_w.reshape(1, HEAD_W),
      *([cache_k] * pages), *([cache_v] * pages))


def _stack_lane_blocks(x):
    return jnp.concatenate([x[:, c * LANES:(c + 1) * LANES] for c in range(x.shape[1] // LANES)], axis=0)


def _unstack_lane_blocks(y, n):
    return jnp.concatenate([y[c * n:(c + 1) * n] for c in range(y.shape[0] // n)], axis=1)


def _segsum(x, r2_ref):
    n = x.shape[0]
    xs = _stack_lane_blocks(x)
    hi = xs.astype(BF16)
    lo = (xs - hi.astype(F32)).astype(BF16)
    y = jnp.dot(jnp.concatenate([hi, lo], axis=1), r2_ref[...], preferred_element_type=F32)
    return _unstack_lane_blocks(y, n)


def _rwkv_prep_kernel(p_ref, sp_ref, mu_ref, w0_ref, w2_ref, a0_ref, a2_ref, g2_ref, kk_ref, ka_ref, rk_ref, r2_ref,
                      a_out, c_out, w_out, b_out, k_out, v_out, vkr_out, bon_out, g_out, prev_ref, *, tt):
    ti = pl.program_id(1)

    @pl.when(ti == 0)
    def _():
        prev_ref[...] = sp_ref[0]

    p = p_ref[0]
    prev_row = prev_ref[...]
    if tt > 1:
        rolled = pltpu.roll(p, 1, axis=0)
        row = lax.broadcasted_iota(I32, p.shape, 0)
        p_prev = jnp.where(row == 0, prev_row, rolled)
    else:
        p_prev = prev_row
    prev_ref[...] = p[tt - 1:tt, :]
    pm = p + mu_ref[...] * (p_prev - p)
    r = pm[:, 0:RW_W]
    kx = pm[:, RW_W:2 * RW_W]
    vx = pm[:, 2 * RW_W:3 * RW_W]
    lora = pm[:, 3 * RW_W:3 * RW_W + RW_LORA_W]
    gl = pm[:, 3 * RW_W + RW_LORA_W:RW_PROJ]
    dw = jnp.dot(jnp.tanh(lora).astype(BF16), w2_ref[...], preferred_element_type=F32)
    w = -jax.nn.softplus(-(w0_ref[...] + dw)) - 0.5
    decay = jnp.exp(-jnp.exp(w))
    a = jax.nn.sigmoid(a0_ref[...] + jnp.dot(lora.astype(BF16), a2_ref[...], preferred_element_type=F32))
    g = jnp.dot(jax.nn.sigmoid(gl).astype(BF16), g2_ref[...], preferred_element_type=F32)
    kk = kx * kk_ref[...]
    kkn = kk / jnp.maximum(jnp.sqrt(_segsum(kk * kk, r2_ref)), 1e-12)
    k = kx * (1.0 + (a - 1.0) * ka_ref[...])
    av = -kkn
    bv = kkn * a
    br = _segsum(bv * r, r2_ref)
    kr = _segsum(k * r, r2_ref)
    a_out[0] = av
    c_out[0] = decay * r + av * br
    w_out[0] = decay
    b_out[0] = bv
    k_out[0] = k
    v_out[0] = vx
    vkr_out[0] = vx * kr
    bon_out[0] = _segsum(r * k * rk_ref[...], r2_ref) * vx
    g_out[0] = g


def _rwkv_scan_kernel(a_ref, c_ref, w_ref, b_ref, k_ref, v_ref, s0_ref, r_ref, e_ref, y_ref, so_ref, s_ref,
                      *, nb, tt, nst):
    ti = pl.program_id(1)

    @pl.when(ti == 0)
    def _():
        s_ref[...] = s0_ref[...]

    row = lax.broadcasted_iota(I32, (RW_N, RW_W), 0)
    lane = lax.broadcasted_iota(I32, (RW_N, RW_W), 1)
    diag = (lane & (RW_N - 1)) == row
    rmat = r_ref[...]

    def group(g, carry):
        zacc = jnp.zeros((nb * RW_N, LANES), F32)
        for j in range(nst):
            t = g * nst + j
            states, stacked, pz = [], [], []
            for bb in range(nb):
                s = s_ref[bb]
                states.append(s)
                pa = (s * a_ref[bb, pl.ds(t, 1), :]).astype(BF16)
                pz.append((s * c_ref[bb, pl.ds(t, 1), :]).astype(BF16))
                lv = jnp.where(diag, v_ref[bb, pl.ds(t, 1), :], 0.0).astype(BF16)
                stacked += [_stack_lane_blocks(pa), _stack_lane_blocks(lv)]
            red = jnp.dot(jnp.concatenate(stacked, axis=0), rmat, preferred_element_type=F32)
            n8 = 8 * RW_N
            for bb in range(nb):
                sa = _unstack_lane_blocks(red[(2 * bb) * n8:(2 * bb + 1) * n8], RW_N)
                vb = _unstack_lane_blocks(red[(2 * bb + 1) * n8:(2 * bb + 2) * n8], RW_N)
                s_ref[bb] = (states[bb] * w_ref[bb, pl.ds(t, 1), :] + sa * b_ref[bb, pl.ds(t, 1), :]
                             + vb * k_ref[bb, pl.ds(t, 1), :])
            zacc = zacc + jnp.dot(jnp.concatenate(pz, axis=0), e_ref[j], preferred_element_type=F32)
        for bb in range(nb):
            y_ref[bb, g] = zacc[bb * RW_N:(bb + 1) * RW_N]
        return carry

    lax.fori_loop(0, tt // nst, group, 0)

    @pl.when(ti == pl.num_programs(1) - 1)
    def _():
        so_ref[...] = s_ref[...]


def _rwkv_post_kernel(y_ref, vkr_ref, bon_ref, g_ref, lw_ref, lb_ref, r2_ref, o_ref):
    y = y_ref[...] + vkr_ref[...]
    mu = _segsum(y, r2_ref) * (1.0 / RW_N)
    d = y - mu
    var = _segsum(d * d, r2_ref) * (1.0 / RW_N)
    yn = d * lax.rsqrt(var + RW_LN_EPS) * lw_ref[...] + lb_ref[...]
    o_ref[...] = ((yn + bon_ref[...]) * g_ref[...]).astype(BF16)


def rwkv_mix(prw, shift_prev, wkv0, prm, batch, seq):
    m = batch * seq
    tt = min(seq, 128)
    seg =jnp.kron(jnp.eye(2, dtype=F32), jnp.ones((RW_N, RW_N), F32))
    r2 = jnp.concatenate([seg, seg], axis=0).astype(BF16)
    zeros64 = jnp.zeros((RW_N, RW_W), F32)
    w2p = jnp.concatenate([prm["rw_w2"], zeros64], axis=0).astype(BF16)
    a2p = jnp.concatenate([zeros64, prm["rw_a2"]], axis=0).astype(BF16)
    row = lambda a: a.reshape(1, -1)
    blk_in = pl.BlockSpec((1, tt, RW_PROJ), lambda b, i: (b, i, 0))
    blk_out = pl.BlockSpec((1, tt, RW_W), lambda b, i: (b, i, 0))
    c2 = lambda shape: pl.BlockSpec(shape, lambda b, i: (0,) * len(shape))
    outs = pl.pallas_call(
        functools.partial(_rwkv_prep_kernel, tt=tt),
        grid=(batch, seq // tt),
        in_specs=[blk_in, pl.BlockSpec((1, 1, RW_PROJ), lambda b, i: (b, 0, 0)), c2((1, RW_PROJ)),
                  c2((1, RW_W)), c2((RW_LORA_W, RW_W)), c2((1, RW_W)), c2((RW_LORA_W, RW_W)),
                  c2((RW_GATE_W, RW_W)), c2((1, RW_W)), c2((1, RW_W)), c2((1, RW_W)), c2((2 * LANES, LANES))],
        out_specs=[blk_out] * 9,
        out_shape=[jax.ShapeDtypeStruct((batch, seq, RW_W), F32)] * 9,
        scratch_shapes=[pltpu.VMEM((1, RW_PROJ), F32)],
        compiler_params=_cparams(("parallel", "arbitrary"), 56),
        name="rwkv_prep",
    )(prw.reshape(batch, seq, RW_PROJ), shift_prev.reshape(batch, 1, RW_PROJ), row(prm["rw_mu"]),
      row(prm["rw_w0"]), w2p, row(prm["rw_a0"]), a2p, prm["rw_g2"].astype(BF16), row(prm["rw_k_k"]),
      row(prm["rw_k_a"]), row(prm["rw_r_k"]), r2)
    av, cv, wv, bv, kv, vv, vkr, bon, gv = outs

    nst = min(seq, 8)
    ts = min(seq, 64)
    nb = 4
    n8 = seq // nst
    head = jnp.arange(RW_W, dtype=I32) // RW_N
    ecomp = (head[None, :, None] + RW_HEADS * jnp.arange(nst, dtype=I32)[:, None, None]
             == jnp.arange(LANES, dtype=I32)[None, None, :]).astype(BF16)
    s0 = wkv0.astype(F32).transpose(0, 2, 1, 3).reshape(batch, RW_N, RW_W)
    seq_blk = pl.BlockSpec((nb, ts, RW_W), lambda b, i: (b, i, 0))
    st_blk = pl.BlockSpec((nb, RW_N, RW_W), lambda b, i: (b, 0, 0))
    yc, s_fin = pl.pallas_call(
        functools.partial(_rwkv_scan_kernel, nb=nb, tt=ts, nst=nst),
        grid=(batch // nb, seq // ts),
        in_specs=[seq_blk] * 6 + [st_blk, c2((LANES, LANES)), c2((nst, RW_W, LANES))],
        out_specs=[pl.BlockSpec((nb, ts // nst, RW_N, LANES), lambda b, i: (b, i, 0, 0)), st_blk],
        out_shape=[jax.ShapeDtypeStruct((batch, n8, RW_N, LANES), F32),
                   jax.ShapeDtypeStruct((batch, RW_N, RW_W), F32)],
        scratch_shapes=[pltpu.VMEM((nb, RW_N, RW_W), F32)],
        compiler_params=_cparams(("parallel", "arbitrary"), 48),
        name="rwkv_scan",
    )(av, cv, wv, bv, kv, vv, s0, seg.astype(BF16), ecomp)
    yz = yc.reshape(batch, n8, RW_N, LANES // RW_HEADS, RW_HEADS)[:, :, :, :nst, :]
    yz = yz.transpose(0, 1, 3, 4, 2).reshape(m, RW_W)
    wkv_new = s_fin.reshape(batch, RW_N, RW_HEADS, RW_N).transpose(0, 2, 1, 3)

    tm = min(m, ROW_TILE)
    rows = pl.BlockSpec((tm, RW_W), lambda i: (i, 0))
    y = pl.pallas_call(
        _rwkv_post_kernel,
        grid=(m // tm,),
        in_specs=[rows] * 4 + [_full((1, RW_W)), _full((1, RW_W)), _full((2 * LANES, LANES))],
        out_specs=rows,
        out_shape=jax.ShapeDtypeStruct((m, RW_W), BF16),
        compiler_params=_cparams(("parallel",), 48),
        name="rwkv_post",
    )(yz, vkr.reshape(m, RW_W), bon.reshape(m, RW_W), gv.reshape(m, RW_W), row(prm["rw_ln_w"]),
      row(prm["rw_ln_b"]), r2)
    return y, wkv_new


def _merge_kernel(ya_ref, yb_ref, ga_ref, gb_ref, ba_ref, bb_ref, wa_ref, wb_ref, o_ref):
    pa = jnp.dot(ya_ref[...].astype(BF16), wa_ref[...], preferred_element_type=F32)
    pb = jnp.dot(yb_ref[...], wb_ref[...], preferred_element_type=F32)
    ga = jax.nn.sigmoid(ga_ref[...] + ba_ref[...])
    gb = jax.nn.sigmoid(gb_ref[...] + bb_ref[...])
    o_ref[...] = (ga * pa + gb * pb).astype(BF16)


def merge_branches(y_att, y_rw, g_lin, b_gate, wa, wb):
    m = y_att.shape[0]
    tm = min(m, ROW_TILE)
    d = D_MODEL
    return pl.pallas_call(
        _merge_kernel,
        grid=(m // tm,),
        in_specs=[pl.BlockSpec((tm, ATT_W), lambda i: (i, 0)), pl.BlockSpec((tm, RW_W), lambda i: (i, 0)),
                  pl.BlockSpec((tm, d), lambda i: (i, 0)), pl.BlockSpec((tm, d), lambda i: (i, 1)),
                  pl.BlockSpec((1, d), lambda i: (0, 0)), pl.BlockSpec((1, d), lambda i: (0, 1)),
                  _full((ATT_W, d)), _full((RW_W, d))],
        out_specs=pl.BlockSpec((tm, d), lambda i: (i, 0)),
        out_shape=jax.ShapeDtypeStruct((m, d), BF16),
        compiler_params=_cparams(("parallel",), 48),
        name="merge_branches",
    )(y_att, y_rw, g_lin, g_lin, b_gate.reshape(1, 2 * d), b_gate.reshape(1, 2 * d), wa, wb)


def _outproj_router_kernel(mg_ref, wo_ref, x_ref, nw_ref, wr_ref, br_ref, h_ref, hn_ref, eid_ref, ew_ref):
    h = x_ref[...] + jnp.dot(mg_ref[...], wo_ref[...], preferred_element_type=F32)
    h_ref[...] = h
    ms = jnp.mean(h * h, axis=-1, keepdims=True)
    hn = h * lax.rsqrt(ms + NORM_EPS) * nw_ref[...]
    hn_ref[...] = hn
    lg = jnp.dot(hn, wr_ref[...], preferred_element_type=F32, precision=lax.Precision.HIGHEST) + br_ref[...]
    lane = lax.broadcasted_iota(I32, lg.shape, 1)
    lane_f = lane.astype(F32)
    big = float(LANES)
    lgm = jnp.where(lane < N_GROUPS, lg, NEG)
    gmax = jnp.max(lgm, axis=-1, keepdims=True)
    gidx = jnp.min(jnp.where(lgm == gmax, lane_f, big), axis=-1, keepdims=True)
    g_w = 1.0 / jnp.sum(jnp.exp(lgm - gmax), axis=-1, keepdims=True)
    lane_group = ((lane >> 3) - 1).astype(F32)
    in_group = (lane >= N_GROUPS) & (lane < N_GROUPS + N_EXPERTS) & (lane_group == gidx)
    le = jnp.where(in_group, lg, NEG)
    m1 = jnp.max(le, axis=-1, keepdims=True)
    i1 = jnp.min(jnp.where(le == m1, lane_f, big), axis=-1, keepdims=True)
    le2 = jnp.where(lane_f == i1, NEG, le)
    m2 = jnp.max(le2, axis=-1, keepdims=True)
    i2 = jnp.min(jnp.where(le2 == m2, lane_f, big), axis=-1, keepdims=True)
    e2 = jnp.exp(m2 - m1)
    w1 = g_w / (1.0 + e2)
    w2 = g_w * e2 / (1.0 + e2)
    col = lax.broadcasted_iota(I32, eid_ref.shape, 1)
    eid_ref[...] = jnp.where(col == 0, i1, i2).astype(I32) - N_GROUPS
    ew_ref[...] = jnp.where(col == 0, w1, w2)


def outproj_router(merged, w_out, x, norm_w, w_router, b_router):
    m, d = x.shape
    tm = min(m, ROW_TILE)
    rows = lambda w: pl.BlockSpec((tm, w), lambda i: (i, 0))
    return pl.pallas_call(
        _outproj_router_kernel,
        grid=(m // tm,),
        in_specs=[rows(d), _full((d, d)), rows(d), _full((1, d)), _full((d, LANES)), _full((1, LANES))],
        out_specs=[rows(d), rows(d), rows(2), rows(2)],
        out_shape=[jax.ShapeDtypeStruct((m, d), F32), jax.ShapeDtypeStruct((m, d), F32),
                   jax.ShapeDtypeStruct((m, 2), I32), jax.ShapeDtypeStruct((m, 2), F32)],
        compiler_params=_cparams(("parallel",), 48),
        name="outproj_router",
    )(merged, w_out, x, norm_w.reshape(1, d), w_router, b_router)


def _moe_kernel(starts_ref, counts_ref, src_ref, hp_ref, hs_ref, wg_ref, wu_ref, wd_ref, yp_ref, ys_ref,
                wgb, wub, wdb, xbuf, ybuf, sem_in, sem_out, *, n_prompt, n_sample):
    e = pl.program_id(0)

    @pl.when(e == 0)
    def _():
        xbuf[...] = jnp.zeros(xbuf.shape, F32)

    n = counts_ref[e]
    start = starts_ref[e]

    def row_in(src_ref_, tok, r):
        return pltpu.make_async_copy(src_ref_.at[pl.ds(tok, 1)], xbuf.at[pl.ds(r, 1)], sem_in)

    def row_out(dst_ref, dst, r):
        return pltpu.make_async_copy(ybuf.at[pl.ds(r, 1)], dst_ref.at[pl.ds(dst, 1)], sem_out)

    @pl.when(n > 0)
    def _():
        wgb[...] = wg_ref[0].astype(BF16)
        wub[...] = wu_ref[0].astype(BF16)
        wdb[...] = wd_ref[0].astype(BF16)

        def block(bi, carry):
            r0 = start + bi * MOE_BLK
            nv = jnp.minimum(MOE_BLK, n - bi * MOE_BLK)

            def gather(r, c):
                tok = src_ref[r0 + r] >> 1

                @pl.when(tok < n_prompt)
                def _():
                    row_in(hp_ref, tok, r).start()

                @pl.when(tok >= n_prompt)
                def _():
                    row_in(hs_ref, tok - n_prompt, r).start()
                return c

            def gather_wait(r, c):
                row_in(hp_ref, 0, r).wait()
                return c

            def scatter_wait(r, c):
                row_out(yp_ref, 0, r).wait()
                return c

            lax.fori_loop(0, nv, gather, 0)
            lax.fori_loop(0, nv, gather_wait, 0)
            x = xbuf[...].astype(BF16)
            hg = jnp.dot(x, wgb[...], preferred_element_type=F32)
            hu = jnp.dot(x, wub[...], preferred_element_type=F32)
            hd = (hg * jax.nn.sigmoid(hg) * hu).astype(BF16)
            ybuf[...] = jnp.dot(hd, wdb[...], preferred_element_type=F32)

            def scatter(r, c):
                src = src_ref[r0 + r]
                tok = src >> 1
                slot = src & 1

                @pl.when(tok < n_prompt)
                def _():
                    row_out(yp_ref, slot * n_prompt + tok, r).start()

                @pl.when(tok >= n_prompt)
                def _():
                    row_out(ys_ref, slot * n_sample + tok - n_prompt, r).start()
                return c

            lax.fori_loop(0, nv, scatter, 0)
            lax.fori_loop(0, nv, scatter_wait, 0)
            return carry

        lax.fori_loop(0, (n + MOE_BLK - 1) // MOE_BLK, block, 0)


def moe_experts(hn_p, hn_s, eid, w_gate, w_up, w_down):
    n_prompt, d = hn_p.shape
    n_sample = hn_s.shape[0]
    n_rows = eid.shape[0] * 2
    flat_e = eid.reshape(n_rows)
    onehot = (flat_e[:, None] == jnp.arange(N_EXPERTS, dtype=I32)[None, :]).astype(I32)
    counts = jnp.sum(onehot, axis=0)
    starts = jnp.cumsum(counts) - counts
    rank = jnp.sum(onehot * (jnp.cumsum(onehot, axis=0) - 1), axis=1)
    pos = starts[flat_e] + rank
    src = jnp.zeros((n_rows,), I32).at[pos].set(jnp.arange(n_rows, dtype=I32))
    any_spec = pl.BlockSpec(memory_space=pl.ANY)
    grid_spec = pltpu.PrefetchScalarGridSpec(
        num_scalar_prefetch=3,
        grid=(N_EXPERTS,),
        in_specs=[any_spec, any_spec,
                  pl.BlockSpec((1, d, D_FF), lambda e, *_: (e, 0, 0)),
                  pl.BlockSpec((1, d, D_FF), lambda e, *_: (e, 0, 0)),
                  pl.BlockSpec((1, D_FF, d), lambda e, *_: (e, 0, 0))],
        out_specs=[any_spec, any_spec],
        scratch_shapes=[pltpu.VMEM((d, D_FF), BF16), pltpu.VMEM((d, D_FF), BF16), pltpu.VMEM((D_FF, d), BF16),
                        pltpu.VMEM((MOE_BLK, d), F32), pltpu.VMEM((MOE_BLK, d), F32),
                        pltpu.SemaphoreType.DMA(()), pltpu.SemaphoreType.DMA(())],
    )
    return pl.pallas_call(
        functools.partial(_moe_kernel, n_prompt=n_prompt, n_sample=n_sample),
        grid_spec=grid_spec,
        out_shape=[jax.ShapeDtypeStruct((2 * n_prompt, d), F32), jax.ShapeDtypeStruct((2 * n_sample, d), F32)],
        compiler_params=_cparams(("arbitrary",), 56),
        name="moe_experts",
    )(starts, counts, src, hn_p, hn_s, w_gate, w_up, w_down)


def _final_kernel(h_ref, y0_ref, y1_ref, ew_ref, nw_ref, o_ref):
    ew = ew_ref[...]
    x = h_ref[...] + ew[:, 0:1] * y0_ref[...] + ew[:, 1:2] * y1_ref[...]
    ms = jnp.mean(x * x, axis=-1, keepdims=True)
    o_ref[...] = x * lax.rsqrt(ms + NORM_EPS) * nw_ref[...]


def combine_final(h, y_rows, ew, norm_w):
    m, d = h.shape
    tm = min(m, ROW_TILE)
    o1 = m // tm
    return pl.pallas_call(
        _final_kernel,
        grid=(m // tm,),
        in_specs=[pl.BlockSpec((tm, d), lambda i: (i, 0)),
                  pl.BlockSpec((tm, d), lambda i: (i, 0)),
                  pl.BlockSpec((tm, d), lambda i: (i + o1, 0)),
                  pl.BlockSpec((tm, 2), lambda i: (i, 0)), _full((1, d))],
        out_specs=pl.BlockSpec((tm, d), lambda i: (i, 0)),
        out_shape=jax.ShapeDtypeStruct((m, d), F32),
        compiler_params=_cparams(("parallel",), 48),
        name="combine_final",
    )(h, y_rows, y_rows, ew, norm_w.reshape(1, d))


_RW_COL0 = 3 * ATT_W
_GATE_COL0 = _RW_COL0 + RW_PROJ


def _in_proj(x2, prm, q_dtype):
    xn = rmsnorm_bf16(x2, prm["norm_mix_w"])
    w_in = prm["w_in"]
    q = matmul_cols(xn, w_in, 0, ATT_W, q_dtype)
    k = matmul_cols(xn, w_in, ATT_W, ATT_W, F32)
    v = matmul_cols(xn, w_in, 2 * ATT_W, ATT_W, F32)
    prw = matmul_cols(xn, w_in, _RW_COL0, RW_PROJ, F32)
    g_lin = matmul_cols(xn, w_in, _GATE_COL0, 2 * D_MODEL, F32)
    return q, k, v, prw, g_lin


def _mixer_tail(x2, y_att, y_rw, g_lin, prm):
    merged = merge_branches(y_att, y_rw, g_lin, prm["b_gate"], prm["wa_bf16"], prm["wb_bf16"])
    return outproj_router(merged, prm["wo_bf16"], x2, prm["norm_ffn_w"], prm["w_router"], prm["b_router"])


def kernel(x_prompt, x_sample, cache_k, cache_v, state_wkv, state_shift, page_table, norm_mix_w, w_in, b_gate, lambda_q1, lambda_k1, lambda_q2, lambda_k2, subln_w, rel_bias, rw_mu, rw_w0, rw_w2, rw_a0, rw_a2, rw_g2, rw_k_k, rw_k_a, rw_r_k, rw_ln_w, rw_ln_b, w_branch_a, w_branch_b, w_out, norm_ffn_w, w_router_g, b_router_g, w_router_e, b_router_e, w_gate_e, w_up_e, w_down_e, norm_final_w):
    assert w_in.shape[0] == 1, "single-layer trunk"
    bp, sp, d = x_prompt.shape
    bs, ss, _ = x_sample.shape
    assert ss == 1 and d == D_MODEL
    n_pool = cache_k.shape[1]
    pad_lanes = LANES - N_GROUPS - N_EXPERTS
    prm = {
        "norm_mix_w": norm_mix_w[0], "w_in": w_in[0], "b_gate": b_gate[0],
        "rw_mu": rw_mu[0], "rw_w0": rw_w0[0], "rw_w2": rw_w2[0], "rw_a0": rw_a0[0], "rw_a2": rw_a2[0],
        "rw_g2": rw_g2[0], "rw_k_k": rw_k_k[0], "rw_k_a": rw_k_a[0], "rw_r_k": rw_r_k[0].reshape(RW_W),
        "rw_ln_w": rw_ln_w[0], "rw_ln_b": rw_ln_b[0],
        "wa_bf16": w_branch_a[0].astype(BF16), "wb_bf16": w_branch_b[0].astype(BF16),
        "wo_bf16": w_out[0].astype(BF16), "norm_ffn_w": norm_ffn_w[0],
        "w_router": jnp.concatenate([w_router_g[0], w_router_e[0], jnp.zeros((d, pad_lanes), F32)], axis=1),
        "b_router": jnp.concatenate([b_router_g[0], b_router_e[0], jnp.zeros((pad_lanes,), F32)]).reshape(1, LANES),
    }
    lamv = jnp.stack([lambda_q1[0], lambda_k1[0], lambda_q2[0], lambda_k2[0]]).astype(F32)
    sub_w = subln_w[0]

    mp = bp * sp
    xp2 = x_prompt.reshape(mp, d)
    q_p, k_p, v_p, prw_p, g_p = _in_proj(xp2, prm, BF16)
    att_p = attn_prompt(q_p, k_p, v_p, lamv, rel_bias, sub_w, bp, sp)
    rw_p, wkv_p = rwkv_mix(prw_p, jnp.zeros((bp, RW_PROJ), F32), jnp.zeros((bp, RW_HEADS, RW_N, RW_N), F32),
                           prm, bp, sp)
    h_p, hn_p, eid_p, ew_p = _mixer_tail(xp2, att_p, rw_p, g_p, prm)

    xs2 = x_sample.reshape(bs, d)
    q_s, k_s, v_s, prw_s, g_s = _in_proj(xs2, prm, F32)
    heads = lambda a: a.reshape(bs, ATT_HEADS, HEAD_W)
    att_s = attn_decode(heads(q_s), heads(k_s), heads(v_s),
                        cache_k[0].reshape(n_pool, PAGE * ATT_HEADS, HEAD_W),
                        cache_v[0].reshape(n_pool, PAGE * ATT_HEADS, HEAD_W),
                        page_table, lamv, rel_bias, sub_w).reshape(bs, ATT_W)
    rw_s, wkv_s = rwkv_mix(prw_s, state_shift[0], state_wkv[0], prm, bs, 1)
    h_s, hn_s, eid_s, ew_s = _mixer_tail(xs2, att_s, rw_s, g_s, prm)

    yr_p, yr_s = moe_experts(hn_p, hn_s, jnp.concatenate([eid_p, eid_s], axis=0),
                             w_gate_e[0], w_up_e[0], w_down_e[0])
    y_p = combine_final(h_p, yr_p, ew_p, norm_final_w)
    y_s = combine_final(h_s, yr_s, ew_s, norm_final_w)

    return (y_p.reshape(bp, sp, d), y_s.reshape(bs, 1, d),
            k_p.reshape(1, bp, sp, ATT_HEADS, HEAD_W), v_p.reshape(1, bp, sp, ATT_HEADS, HEAD_W),
            wkv_p.astype(state_wkv.dtype)[None], prw_p.reshape(bp, sp, RW_PROJ)[None, :, -1],
            k_s.reshape(1, bs, 1, ATT_HEADS, HEAD_W), v_s.reshape(1, bs, 1, ATT_HEADS, HEAD_W),
            wkv_s.astype(state_wkv.dtype)[None], prw_s.reshape(1, bs, RW_PROJ))
```

```python
import functools
import math

import jax
import jax.numpy as jnp
from jax import lax
from jax.experimental import pallas as pl
from jax.experimental.pallas import tpu as pltpu

F32 = jnp.float32
BF16 = jnp.bfloat16
I32 = jnp.int32

D_MODEL = 2048
ATT_HEADS = 8
ATT_DK = 64
HEAD_W = 2 * ATT_DK
ATT_W = ATT_HEADS * HEAD_W
ATT_SCALE = ATT_DK ** -0.5
RW_HEADS = 16
RW_N = 64
RW_W = RW_HEADS * RW_N
RW_LORA_W = 128
RW_GATE_W = 128
RW_PROJ = 3 * RW_W + RW_LORA_W + RW_GATE_W
N_BUCKETS = 32
MAX_DISTANCE = 128
PAGE = 128
EXPERTS_PER_GROUP = 8
N_GROUPS = 8
N_EXPERTS = N_GROUPS * EXPERTS_PER_GROUP
D_FF = 512
NORM_EPS = 1e-6
SUBLN_EPS = 1e-5
RW_LN_EPS = 64e-5
LAM_INIT = 0.8 - 0.6 * math.exp(-0.3 * 0)
NEG = -1e30
LANES = 128
MXU_DIM = 256
MOE_BLK = 128
MOE_UNROLL = 16
ROW_TILE = 256
MIB = 1024 * 1024


def _cparams(sem, vmem_mib):
    return pltpu.CompilerParams(dimension_semantics=sem, vmem_limit_bytes=vmem_mib * MIB)


def _full(shape):
    nd = len(shape)
    return pl.BlockSpec(shape, lambda *_: (0,) * nd)


def _rmsnorm_bf16_kernel(x_ref, w_ref, o_ref):
    x = x_ref[...]
    ms = jnp.mean(x * x, axis=-1, keepdims=True)
    o_ref[...] = (x * lax.rsqrt(ms + NORM_EPS) * w_ref[...]).astype(BF16)


def rmsnorm_bf16(x, w):
    m, d = x.shape
    tm = min(m, 512)
    return pl.pallas_call(
        _rmsnorm_bf16_kernel,
        grid=(m // tm,),
        in_specs=[pl.BlockSpec((tm, d), lambda i: (i, 0)), _full((1, d))],
        out_specs=pl.BlockSpec((tm, d), lambda i: (i, 0)),
        out_shape=jax.ShapeDtypeStruct((m, d), BF16),
        compiler_params=_cparams(("parallel",), 32),
        name="rmsnorm_bf16",
    )(x, w.reshape(1, d))


def _matmul_kernel(x_ref, w_ref, o_ref):
    o_ref[...] = jnp.dot(x_ref[...], w_ref[...].astype(BF16),
                         preferred_element_type=F32).astype(o_ref.dtype)


def matmul_cols(x, w, col0, ncols, out_dtype):
    m, k = x.shape
    tm = next(t for t in (2048, 1024, 512, 256, m) if m % t == 0)
    tn = next(t for t in (512, 256, 128) if col0 % t == 0 and ncols % t == 0)
    off = col0 // tn
    return pl.pallas_call(
        _matmul_kernel,
        grid=(m // tm, ncols // tn),
        in_specs=[pl.BlockSpec((tm, k), lambda i, j: (i, 0)),
                  pl.BlockSpec((k, tn), lambda i, j: (0, j + off))],
        out_specs=pl.BlockSpec((tm, tn), lambda i, j: (i, j)),
        out_shape=jax.ShapeDtypeStruct((m, ncols), out_dtype),
        compiler_params=_cparams(("parallel", "arbitrary"), 48),
        name="in_proj",
    )(x, w)


def _rel_bucket(dist):
    n = jnp.maximum(dist, 0)
    max_exact = N_BUCKETS // 2
    large = max_exact + (jnp.log(jnp.maximum(n, 1).astype(F32) / max_exact)
                         / math.log(MAX_DISTANCE / max_exact) * (N_BUCKETS - max_exact)).astype(I32)
    large = jnp.minimum(large, N_BUCKETS - 1)
    return jnp.where(n < max_exact, n, large)


def _bias_by_distance(rel_bias, n):
    d = jnp.arange(n, dtype=I32)
    table = rel_bias.astype(F32)[_rel_bucket(d)]
    far = rel_bias.astype(F32)[N_BUCKETS - 1]
    return (table - far[None, :]).T


def _toeplitz(vals, t):
    h = vals.shape[0]
    w = jnp.concatenate([vals[:, ::-1], jnp.zeros((h, 1), vals.dtype)], axis=1)
    flat = jnp.tile(w, (1, t))[:, :t * (2 * t - 1)]
    return flat.reshape(h, t, 2 * t - 1)[:, :, t - 1:]


def _lambda(lamv_ref):
    lv = lamv_ref[...]
    s1 = jnp.sum(lv[0:1] * lv[1:2], axis=-1, keepdims=True)
    s2 = jnp.sum(lv[2:3] * lv[3:4], axis=-1, keepdims=True)
    return jnp.exp(s1) - jnp.exp(s2) + LAM_INIT


def _split_halves(q):
    lane = lax.broadcasted_iota(I32, q.shape, 1)
    return jnp.concatenate([jnp.where(lane < ATT_DK, q, 0.0), jnp.where(lane >= ATT_DK, q, 0.0)], axis=0)


def _attn_prompt_kernel(lamv_ref, q_ref, k_ref, v_ref, bd_ref, bp_ref, sw_ref, o_ref, kb_ref, vb_ref, *, t, nq):
    kb_ref[...] = k_ref[...].astype(BF16)
    vb_ref[...] = v_ref[...].astype(BF16)
    lam = _lambda(lamv_ref)
    bd = bd_ref[0]
    bp = bp_ref[0]
    bd2 = jnp.concatenate([bd, bd], axis=0)
    bp2 = jnp.concatenate([bp, bp], axis=0)
    sw = sw_ref[...] * (1.0 - LAM_INIT)

    def block(q2, j0, bias, carry):
        m, l, acc = carry
        kj = kb_ref[pl.ds(j0, t), :]
        s = lax.dot_general(q2, kj, (((1,), (1,)), ((), ())), preferred_element_type=F32)
        if bias is not None:
            s = s + bias
        mn = jnp.maximum(m, jnp.max(s, axis=-1, keepdims=True))
        alpha = jnp.exp(m - mn)
        e = jnp.exp(s - mn)
        l = alpha * l + jnp.sum(e, axis=-1, keepdims=True)
        acc = alpha * acc + jnp.dot(e.astype(BF16), vb_ref[pl.ds(j0, t), :], preferred_element_type=F32)
        return mn, l, acc

    for i in range(nq):
        q = q_ref[i * t:(i + 1) * t, :].astype(F32) * ATT_SCALE
        q2 = _split_halves(q).astype(BF16)
        carry = (jnp.full((2 * t, 1), NEG, F32), jnp.zeros((2 * t, 1), F32), jnp.zeros((2 * t, HEAD_W), F32))
        if i >= 2:
            carry = lax.fori_loop(
                0, i - 1, lambda j, c: block(q2, pl.multiple_of(j * t, t), None, c), carry)
        if i >= 1:
            carry = block(q2, (i - 1) * t, bp2, carry)
        m, l, acc = block(q2, i * t, bd2, carry)
        o = acc[:t] / l[:t] - lam * (acc[t:] / l[t:])
        ms = jnp.mean(o * o, axis=-1, keepdims=True)
        o_ref[i * t:(i + 1) * t, :] = o * lax.rsqrt(ms + SUBLN_EPS) * sw


def attn_prompt(q, k, v, lamv, rel_bias, subln_w, batch, seq):
    t = min(seq, 256)
    nq = seq // t
    assert t >= LANES, "blocks two or more away must lie beyond the last distinct distance bucket"
    bd0 = _bias_by_distance(rel_bias, 2 * t)
    bdiag = _toeplitz(jnp.concatenate([jnp.full((ATT_HEADS, t - 1), NEG, F32), bd0[:, :t]], axis=1), t)
    bprev = _toeplitz(bd0[:, 1:], t)
    blk = lambda: pl.BlockSpec((seq, HEAD_W), lambda b, h: (b, h))
    return pl.pallas_call(
        functools.partial(_attn_prompt_kernel, t=t, nq=nq),
        grid=(batch, ATT_HEADS),
        in_specs=[_full((4, ATT_DK)), blk(), blk(), blk(),
                  pl.BlockSpec((1, t, t), lambda b, h: (h, 0, 0)),
                  pl.BlockSpec((1, t, t), lambda b, h: (h, 0, 0)),
                  _full((1, HEAD_W))],
        out_specs=blk(),
        out_shape=jax.ShapeDtypeStruct((batch * seq, ATT_W), F32),
        scratch_shapes=[pltpu.VMEM((seq, HEAD_W), BF16), pltpu.VMEM((seq, HEAD_W), BF16)],
        compiler_params=_cparams(("parallel", "parallel"), 48),
        name="attn_prompt",
    )(lamv, q, k, v, bdiag, bprev, subln_w.reshape(1, HEAD_W))


def _attn_decode_kernel(pt_ref, lamv_ref, q_ref, kn_ref, vn_ref, bias_ref, bnew_ref, sw_ref, *rest, pages):
    del pt_ref
    k_refs = rest[:pages]
    v_refs = rest[pages:2 * pages]
    o_ref = rest[2 * pages]
    m_ref, l_ref, acc_ref = rest[2 * pages + 1:]
    c = pl.program_id(1)
    last = pl.num_programs(1) - 1

    @pl.when(c == 0)
    def _():
        m_ref[...] = jnp.full(m_ref.shape, NEG, F32)
        l_ref[...] = jnp.zeros(l_ref.shape, F32)
        acc_ref[...] = jnp.zeros(acc_ref.shape, F32)

    q16 = _split_halves(q_ref[0] * ATT_SCALE)
    row_head = lax.broadcasted_iota(I32, (2 * ATT_HEADS, 1), 0) & (ATT_HEADS - 1)
    qh = [jnp.where(row_head == h, q16, 0.0).astype(BF16) for h in range(ATT_HEADS)]

    def head_rows(ref, h):
        return ref[0, pl.ds(h, PAGE, stride=ATT_HEADS), :]

    s_parts = []
    for pp in range(pages // 2):
        sp = None
        for h in range(ATT_HEADS):
            k2 = jnp.concatenate([head_rows(k_refs[2 * pp], h), head_rows(k_refs[2 * pp + 1], h)],
                                 axis=0).astype(BF16)
            d = lax.dot_general(qh[h], k2, (((1,), (1,)), ((), ())), preferred_element_type=F32)
            sp = d if sp is None else sp + d
        s_parts.append(sp)
    s = jnp.concatenate(s_parts, axis=1)
    s = s + bias_ref[...] * (c == last).astype(F32)
    m = m_ref[...]
    mn = jnp.maximum(m, jnp.max(s, axis=-1, keepdims=True))
    alpha = jnp.exp(m - mn)
    e = jnp.exp(s - mn)
    l_ref[...] = alpha * l_ref[...] + jnp.sum(e, axis=-1, keepdims=True)
    m_ref[...] = mn
    eb = e.astype(BF16)
    pv = jnp.zeros((2 * ATT_HEADS, HEAD_W), F32)
    for pp in range(pages // 2):
        ep = eb[:, pp * 2 * PAGE:(pp + 1) * 2 * PAGE]
        for h in range(ATT_HEADS):
            v2 = jnp.concatenate([head_rows(v_refs[2 * pp], h), head_rows(v_refs[2 * pp + 1], h)],
                                 axis=0).astype(BF16)
            r = jnp.dot(ep, v2, preferred_element_type=F32)
            pv = pv + jnp.where(row_head == h, r, 0.0)
    acc_ref[...] = alpha * acc_ref[...] + pv

    @pl.when(c == last)
    def _():
        lam = _lambda(lamv_ref)
        m0 = m_ref[...]
        k16 = jnp.concatenate([kn_ref[0], kn_ref[0]], axis=0)
        v16 = jnp.concatenate([vn_ref[0], vn_ref[0]], axis=0)
        s_new = jnp.sum(q16 * k16, axis=-1, keepdims=True) + bnew_ref[...]
        m1 = jnp.maximum(m0, s_new)
        a1 = jnp.exp(m0 - m1)
        e_new = jnp.exp(s_new - m1)
        l1 = a1 * l_ref[...] + e_new
        acc1 = a1 * acc_ref[...] + e_new * v16
        o = acc1[:ATT_HEADS] / l1[:ATT_HEADS] - lam * (acc1[ATT_HEADS:] / l1[ATT_HEADS:])
        ms = jnp.mean(o * o, axis=-1, keepdims=True)
        o_ref[0] = o * lax.rsqrt(ms + SUBLN_EPS) * (sw_ref[...] * (1.0 - LAM_INIT))


def attn_decode(q, k_new, v_new, cache_k, cache_v, page_table, lamv, rel_bias, subln_w):
    nb, n_pages = page_table.shape
    pages = next(g for g in (8, 4, 2) if n_pages % g == 0)
    n_chunks = n_pages // pages
    width = pages * PAGE
    bd0 = _bias_by_distance(rel_bias, width + 1)
    near = bd0[:, width - jnp.arange(width, dtype=I32)]
    bias_last = jnp.concatenate([near, near], axis=0)
    bnew = jnp.concatenate([bd0[:, 0:1], bd0[:, 0:1]], axis=0)
    row = lambda: pl.BlockSpec((1, ATT_HEADS, HEAD_W), lambda b, c, pt: (b, 0, 0))
    page = lambda g: pl.BlockSpec((1, PAGE * ATT_HEADS, HEAD_W), lambda b, c, pt: (pt[b, c * pages + g], 0, 0))
    const = lambda shape: pl.BlockSpec(shape, lambda b, c, pt: (0,) * len(shape))
    grid_spec = pltpu.PrefetchScalarGridSpec(
        num_scalar_prefetch=1,
        grid=(nb, n_chunks),
        in_specs=[const((4, ATT_DK)), row(), row(), row(), const((2 * ATT_HEADS, width)),
                  const((2 * ATT_HEADS, 1)), const((1, HEAD_W))]
                 + [page(g) for g in range(pages)] + [page(g) for g in range(pages)],
        out_specs=row(),
        scratch_shapes=[pltpu.VMEM((2 * ATT_HEADS, 1), F32), pltpu.VMEM((2 * ATT_HEADS, 1), F32),
                        pltpu.VMEM((2 * ATT_HEADS, HEAD_W), F32)],
    )
    return pl.pallas_call(
        functools.partial(_attn_decode_kernel, pages=pages),
        grid_spec=grid_spec,
        out_shape=jax.ShapeDtypeStruct((nb, ATT_HEADS, HEAD_W), F32),
        compiler_params=_cparams(("parallel", "arbitrary"), 48),
        name="attn_decode",
    )(page_table, lamv, q, k_new, v_new, bias_last, bnew, subln_w.reshape(1, HEAD_W),
      *([cache_k] * pages), *([cache_v] * pages))


def _stack_lane_blocks(x, width=LANES):
    return jnp.concatenate([x[:, c * width:(c + 1) * width] for c in range(x.shape[1] // width)], axis=0)


def _unstack_lane_blocks(y, n):
    return jnp.concatenate([y[c * n:(c + 1) * n] for c in range(y.shape[0] // n)], axis=1)


def _segsum(x, r2_ref):
    n = x.shape[0]
    xs = _stack_lane_blocks(x)
    hi = xs.astype(BF16)
    lo = (xs - hi.astype(F32)).astype(BF16)
    y = jnp.dot(jnp.concatenate([hi, lo], axis=1), r2_ref[...], preferred_element_type=F32)
    return _unstack_lane_blocks(y, n)


def _rwkv_prep_kernel(p_ref, sp_ref, mu_ref, w0_ref, w2_ref, a0_ref, a2_ref, g2_ref, kk_ref, ka_ref, rk_ref, r2_ref,
                      a_out, c_out, w_out, b_out, k_out, v_out, vkr_out, bon_out, g_out, prev_ref, *, tt):
    ti = pl.program_id(1)

    @pl.when(ti == 0)
    def _():
        prev_ref[...] = sp_ref[0]

    p = p_ref[0]
    prev_row = prev_ref[...]
    if tt > 1:
        rolled = pltpu.roll(p, 1, axis=0)
        row = lax.broadcasted_iota(I32, p.shape, 0)
        p_prev = jnp.where(row == 0, prev_row, rolled)
    else:
        p_prev = prev_row
    prev_ref[...] = p[tt - 1:tt, :]
    pm = p + mu_ref[...] * (p_prev - p)
    r = pm[:, 0:RW_W]
    kx = pm[:, RW_W:2 * RW_W]
    vx = pm[:, 2 * RW_W:3 * RW_W]
    lora = pm[:, 3 * RW_W:3 * RW_W + RW_LORA_W]
    gl = pm[:, 3 * RW_W + RW_LORA_W:RW_PROJ]
    dw = jnp.dot(jnp.tanh(lora).astype(BF16), w2_ref[...], preferred_element_type=F32)
    w = -jax.nn.softplus(-(w0_ref[...] + dw)) - 0.5
    decay = jnp.exp(-jnp.exp(w))
    a = jax.nn.sigmoid(a0_ref[...] + jnp.dot(lora.astype(BF16), a2_ref[...], preferred_element_type=F32))
    g = jnp.dot(jax.nn.sigmoid(gl).astype(BF16), g2_ref[...], preferred_element_type=F32)
    kk = kx * kk_ref[...]
    kkn = kk / jnp.maximum(jnp.sqrt(_segsum(kk * kk, r2_ref)), 1e-12)
    k = kx * (1.0 + (a - 1.0) * ka_ref[...])
    av = -kkn
    bv = kkn * a
    br = _segsum(bv * r, r2_ref)
    kr = _segsum(k * r, r2_ref)
    a_out[0] = av
    c_out[0] = decay * r + av * br
    w_out[0] = decay
    b_out[0] = bv
    k_out[0] = k
    v_out[0] = vx
    vkr_out[0] = vx * kr
    bon_out[0] = _segsum(r * k * rk_ref[...], r2_ref) * vx
    g_out[0] = g


def _rwkv_scan_kernel(a_ref, c_ref, w_ref, b_ref, k_ref, v_ref, s0_ref, r_ref, e_ref, y_ref, so_ref, s_ref,
                      *, nb, tt, nst, v_terms):
    ti = pl.program_id(1)

    @pl.when(ti == 0)
    def _():
        s_ref[...] = s0_ref[...]

    row = lax.broadcasted_iota(I32, (RW_N, RW_W), 0)
    lane = lax.broadcasted_iota(I32, (RW_N, RW_W), 1)
    diag = (lane & (RW_N - 1)) == row
    rmat = r_ref[...]
    width = rmat.shape[0]
    per = (RW_W // width) * RW_N

    def row(ref, bb, t):
        return ref[bb, pl.ds(t, 1), :]

    def group(g, carry):
        zacc = jnp.zeros((nb * RW_N, LANES), F32)
        for j in range(nst):
            t = g * nst + j
            states, stacked, pz = [], [], []
            for bb in range(nb):
                s = s_ref[bb]
                states.append(s)
                pa = (s * row(a_ref, bb, t)).astype(BF16)
                pz.append((s * row(c_ref, bb, t)).astype(BF16))
                v = row(v_ref, bb, t)
                v_hi = jnp.where(diag, v, 0.0).astype(BF16)
                terms = [pa, v_hi]
                if v_terms == 2:
                    terms.append(jnp.where(diag, v - v_hi.astype(F32), 0.0).astype(BF16))
                stacked += [_stack_lane_blocks(x, width) for x in terms]
            red = jnp.dot(jnp.concatenate(stacked, axis=0), rmat, preferred_element_type=F32)
            nterm = 1 + v_terms
            for bb in range(nb):
                parts = [_unstack_lane_blocks(red[(nterm * bb + i) * per:(nterm * bb + i + 1) * per], RW_N)
                         for i in range(nterm)]
                vb = parts[1] if v_terms == 1 else parts[1] + parts[2]
                s_ref[bb] = (states[bb] * row(w_ref, bb, t) + parts[0] * row(b_ref, bb, t)
                             + vb * row(k_ref, bb, t))
            zacc = zacc + jnp.dot(jnp.concatenate(pz, axis=0), e_ref[j], preferred_element_type=F32)
        for bb in range(nb):
            y_ref[bb, g] = zacc[bb * RW_N:(bb + 1) * RW_N]
        return carry

    lax.fori_loop(0, tt // nst, group, 0)

    @pl.when(ti == pl.num_programs(1) - 1)
    def _():
        so_ref[...] = s_ref[...]


def _rwkv_post_kernel(y_ref, vkr_ref, bon_ref, g_ref, lw_ref, lb_ref, r2_ref, o_ref):
    y = y_ref[...] + vkr_ref[...]
    mu = _segsum(y, r2_ref) * (1.0 / RW_N)
    d = y - mu
    var = _segsum(d * d, r2_ref) * (1.0 / RW_N)
    yn = d * lax.rsqrt(var + RW_LN_EPS) * lw_ref[...] + lb_ref[...]
    o_ref[...] = ((yn + bon_ref[...]) * g_ref[...]).astype(BF16)


def rwkv_mix(prw, shift_prev, wkv0, prm, batch, seq):
    m = batch * seq
    tt = min(seq, 128)
    seg =jnp.kron(jnp.eye(2, dtype=F32), jnp.ones((RW_N, RW_N), F32))
    r2 = jnp.concatenate([seg, seg], axis=0).astype(BF16)
    zeros64 = jnp.zeros((RW_N, RW_W), F32)
    w2p = jnp.concatenate([prm["rw_w2"], zeros64], axis=0).astype(BF16)
    a2p = jnp.concatenate([zeros64, prm["rw_a2"]], axis=0).astype(BF16)
    row = lambda a: a.reshape(1, -1)
    blk_in = pl.BlockSpec((1, tt, RW_PROJ), lambda b, i: (b, i, 0))
    blk_out = pl.BlockSpec((1, tt, RW_W), lambda b, i: (b, i, 0))
    c2 = lambda shape: pl.BlockSpec(shape, lambda b, i: (0,) * len(shape))
    outs = pl.pallas_call(
        functools.partial(_rwkv_prep_kernel, tt=tt),
        grid=(batch, seq // tt),
        in_specs=[blk_in, pl.BlockSpec((1, 1, RW_PROJ), lambda b, i: (b, 0, 0)), c2((1, RW_PROJ)),
                  c2((1, RW_W)), c2((RW_LORA_W, RW_W)), c2((1, RW_W)), c2((RW_LORA_W, RW_W)),
                  c2((RW_GATE_W, RW_W)), c2((1, RW_W)), c2((1, RW_W)), c2((1, RW_W)), c2((2 * LANES, LANES))],
        out_specs=[blk_out] * 9,
        out_shape=[jax.ShapeDtypeStruct((batch, seq, RW_W), F32)] * 9,
        scratch_shapes=[pltpu.VMEM((1, RW_PROJ), F32)],
        compiler_params=_cparams(("parallel", "arbitrary"), 56),
        name="rwkv_prep",
    )(prw.reshape(batch, seq, RW_PROJ), shift_prev.reshape(batch, 1, RW_PROJ), row(prm["rw_mu"]),
      row(prm["rw_w0"]), w2p, row(prm["rw_a0"]), a2p, prm["rw_g2"].astype(BF16), row(prm["rw_k_k"]),
      row(prm["rw_k_a"]), row(prm["rw_r_k"]), r2)
    av, cv, wv, bv, kv, vv, vkr, bon, gv = outs

    nst = min(seq, 8)
    ts = min(seq, 64)
    nb = 4
    n8 = seq // nst
    head = jnp.arange(RW_W, dtype=I32) // RW_N
    ecomp = (head[None, :, None] + RW_HEADS * jnp.arange(nst, dtype=I32)[:, None, None]
             == jnp.arange(LANES, dtype=I32)[None, None, :]).astype(BF16)
    s0 = wkv0.astype(F32).transpose(0, 2, 1, 3).reshape(batch, RW_N, RW_W)
    seg_mxu = jnp.kron(jnp.eye(MXU_DIM // RW_N, dtype=F32), jnp.ones((RW_N, RW_N), F32)).astype(BF16)
    v_terms = 2 if seq == 1 else 1
    seq_blk = pl.BlockSpec((nb, ts, RW_W), lambda b, i: (b, i, 0))
    st_blk = pl.BlockSpec((nb, RW_N, RW_W), lambda b, i: (b, 0, 0))
    yc, s_fin = pl.pallas_call(
        functools.partial(_rwkv_scan_kernel, nb=nb, tt=ts, nst=nst, v_terms=v_terms),
        grid=(batch // nb, seq // ts),
        in_specs=[seq_blk] * 6 + [st_blk, c2((MXU_DIM, MXU_DIM)), c2((nst, RW_W, LANES))],
        out_specs=[pl.BlockSpec((nb, ts // nst, RW_N, LANES), lambda b, i: (b, i, 0, 0)), st_blk],
        out_shape=[jax.ShapeDtypeStruct((batch, n8, RW_N, LANES), F32),
                   jax.ShapeDtypeStruct((batch, RW_N, RW_W), F32)],
        scratch_shapes=[pltpu.VMEM((nb, RW_N, RW_W), F32)],
        compiler_params=_cparams(("parallel", "arbitrary"), 48),
        name="rwkv_scan",
    )(av, cv, wv, bv, kv, vv, s0, seg_mxu, ecomp)
    yz = yc.reshape(batch, n8, RW_N, LANES // RW_HEADS, RW_HEADS)[:, :, :, :nst, :]
    yz = yz.transpose(0, 1, 3, 4, 2).reshape(m, RW_W)
    wkv_new = s_fin.reshape(batch, RW_N, RW_HEADS, RW_N).transpose(0, 2, 1, 3)

    tm = min(m, ROW_TILE)
    rows = pl.BlockSpec((tm, RW_W), lambda i: (i, 0))
    y = pl.pallas_call(
        _rwkv_post_kernel,
        grid=(m // tm,),
        in_specs=[rows] * 4 + [_full((1, RW_W)), _full((1, RW_W)), _full((2 * LANES, LANES))],
        out_specs=rows,
        out_shape=jax.ShapeDtypeStruct((m, RW_W), BF16),
        compiler_params=_cparams(("parallel",), 48),
        name="rwkv_post",
    )(yz, vkr.reshape(m, RW_W), bon.reshape(m, RW_W), gv.reshape(m, RW_W), row(prm["rw_ln_w"]),
      row(prm["rw_ln_b"]), r2)
    return y, wkv_new


def _merge_kernel(ya_ref, yb_ref, ga_ref, gb_ref, ba_ref, bb_ref, wa_ref, wb_ref, o_ref):
    pa = jnp.dot(ya_ref[...].astype(BF16), wa_ref[...], preferred_element_type=F32)
    pb = jnp.dot(yb_ref[...], wb_ref[...], preferred_element_type=F32)
    ga = jax.nn.sigmoid(ga_ref[...] + ba_ref[...])
    gb = jax.nn.sigmoid(gb_ref[...] + bb_ref[...])
    o_ref[...] = (ga * pa + gb * pb).astype(BF16)


def merge_branches(y_att, y_rw, g_lin, b_gate, wa, wb):
    m = y_att.shape[0]
    tm = min(m, ROW_TILE)
    d = D_MODEL
    return pl.pallas_call(
        _merge_kernel,
        grid=(m // tm,),
        in_specs=[pl.BlockSpec((tm, ATT_W), lambda i: (i, 0)), pl.BlockSpec((tm, RW_W), lambda i: (i, 0)),
                  pl.BlockSpec((tm, d), lambda i: (i, 0)), pl.BlockSpec((tm, d), lambda i: (i, 1)),
                  pl.BlockSpec((1, d), lambda i: (0, 0)), pl.BlockSpec((1, d), lambda i: (0, 1)),
                  _full((ATT_W, d)), _full((RW_W, d))],
        out_specs=pl.BlockSpec((tm, d), lambda i: (i, 0)),
        out_shape=jax.ShapeDtypeStruct((m, d), BF16),
        compiler_params=_cparams(("parallel",), 48),
        name="merge_branches",
    )(y_att, y_rw, g_lin, g_lin, b_gate.reshape(1, 2 * d), b_gate.reshape(1, 2 * d), wa, wb)


def _outproj_router_kernel(mg_ref, wo_ref, x_ref, nw_ref, wr_ref, br_ref, h_ref, hn_ref, eid_ref, ew_ref):
    h = x_ref[...] + jnp.dot(mg_ref[...], wo_ref[...], preferred_element_type=F32)
    h_ref[...] = h
    ms = jnp.mean(h * h, axis=-1, keepdims=True)
    hn = h * lax.rsqrt(ms + NORM_EPS) * nw_ref[...]
    hn_ref[...] = hn
    lg = jnp.dot(hn.astype(BF16), wr_ref[...], preferred_element_type=F32) + br_ref[...]
    lane = lax.broadcasted_iota(I32, lg.shape, 1)
    lane_f = lane.astype(F32)
    big = float(LANES)
    lgm = jnp.where(lane < N_GROUPS, lg, NEG)
    gmax = jnp.max(lgm, axis=-1, keepdims=True)
    gidx = jnp.min(jnp.where(lgm == gmax, lane_f, big), axis=-1, keepdims=True)
    g_w = 1.0 / jnp.sum(jnp.exp(lgm - gmax), axis=-1, keepdims=True)
    lane_group = ((lane >> 3) - 1).astype(F32)
    in_group = (lane >= N_GROUPS) & (lane < N_GROUPS + N_EXPERTS) & (lane_group == gidx)
    le = jnp.where(in_group, lg, NEG)
    m1 = jnp.max(le, axis=-1, keepdims=True)
    i1 = jnp.min(jnp.where(le == m1, lane_f, big), axis=-1, keepdims=True)
    le2 = jnp.where(lane_f == i1, NEG, le)
    m2 = jnp.max(le2, axis=-1, keepdims=True)
    i2 = jnp.min(jnp.where(le2 == m2, lane_f, big), axis=-1, keepdims=True)
    e2 = jnp.exp(m2 - m1)
    w1 = g_w / (1.0 + e2)
    w2 = g_w * e2 / (1.0 + e2)
    col = lax.broadcasted_iota(I32, eid_ref.shape, 1)
    eid_ref[...] = jnp.where(col == 0, i1, i2).astype(I32) - N_GROUPS
    ew_ref[...] = jnp.where(col == 0, w1, w2)


def outproj_router(merged, w_out, x, norm_w, w_router, b_router):
    m, d = x.shape
    tm = min(m, ROW_TILE)
    rows = lambda w: pl.BlockSpec((tm, w), lambda i: (i, 0))
    return pl.pallas_call(
        _outproj_router_kernel,
        grid=(m // tm,),
        in_specs=[rows(d), _full((d, d)), rows(d), _full((1, d)), _full((d, LANES)), _full((1, LANES))],
        out_specs=[rows(d), rows(d), rows(2), rows(2)],
        out_shape=[jax.ShapeDtypeStruct((m, d), F32), jax.ShapeDtypeStruct((m, d), F32),
                   jax.ShapeDtypeStruct((m, 2), I32), jax.ShapeDtypeStruct((m, 2), F32)],
        compiler_params=_cparams(("parallel",), 48),
        name="outproj_router",
    )(merged, w_out, x, norm_w.reshape(1, d), w_router, b_router)


def _moe_kernel(starts_ref, counts_ref, pcounts_ref, src_ref, hp_ref, hs_ref, wg_ref, wu_ref, wd_ref,
                yp_ref, ys_ref, wgb, wub, wdb, xbuf, ybuf, sem_in, sem_out, *, n_prompt, n_sample):
    e = pl.program_id(0)

    @pl.when(e == 0)
    def _():
        xbuf[...] = jnp.zeros(xbuf.shape, F32)

    n = counts_ref[e]
    n_p = pcounts_ref[e]
    start = starts_ref[e]
    nblk = (n + MOE_BLK - 1) // MOE_BLK

    def row_in(src_hbm, tok, slot, r):
        return pltpu.make_async_copy(src_hbm.at[pl.ds(tok, 1)], xbuf.at[slot, pl.ds(r, 1)], sem_in.at[slot])

    def row_out(dst_hbm, dst, r):
        return pltpu.make_async_copy(ybuf.at[pl.ds(r, 1)], dst_hbm.at[pl.ds(dst, 1)], sem_out)

    def for_rows(lo, hi, body):
        whole = (lo == 0) & (hi == MOE_BLK)

        @pl.when(whole)
        def _():
            def chunk(i, c):
                for u in range(MOE_UNROLL):
                    body(i * MOE_UNROLL + u)
                return c
            lax.fori_loop(0, MOE_BLK // MOE_UNROLL, chunk, 0)

        @pl.when(jnp.logical_not(whole))
        def _():
            lax.fori_loop(lo, hi, lambda r, c: (body(r), c)[1], 0)

    def block_rows(bi):
        nv = jnp.minimum(MOE_BLK, n - bi * MOE_BLK)
        nvp = jnp.clip(n_p - bi * MOE_BLK, 0, nv)
        return start + bi * MOE_BLK, nv, nvp

    def gather_start(bi, slot):
        r0, nv, nvp = block_rows(bi)
        for_rows(0, nvp, lambda r: row_in(hp_ref, src_ref[r0 + r] >> 1, slot, r).start())
        for_rows(nvp, nv, lambda r: row_in(hs_ref, (src_ref[r0 + r] >> 1) - n_prompt, slot, r).start())

    def gather_wait(bi, slot):
        _, nv, _ = block_rows(bi)
        for_rows(0, nv, lambda r: row_in(hp_ref, 0, slot, r).wait())

    def scatter_start(bi):
        r0, nv, nvp = block_rows(bi)

        def to_prompt(r):
            src = src_ref[r0 + r]
            row_out(yp_ref, (src & 1) * n_prompt + (src >> 1), r).start()

        def to_sample(r):
            src = src_ref[r0 + r]
            row_out(ys_ref, (src & 1) * n_sample + (src >> 1) - n_prompt, r).start()

        for_rows(0, nvp, to_prompt)
        for_rows(nvp, nv, to_sample)

    def scatter_wait(bi):
        _, nv, _ = block_rows(bi)
        for_rows(0, nv, lambda r: row_out(yp_ref, 0, r).wait())

    @pl.when(n > 0)
    def _():
        gather_start(0, 0)
        wgb[...] = wg_ref[0].astype(BF16)
        wub[...] = wu_ref[0].astype(BF16)
        wdb[...] = wd_ref[0].astype(BF16)

        def block(bi, carry):
            slot = bi & 1
            gather_wait(bi, slot)

            @pl.when(bi + 1 < nblk)
            def _():
                gather_start(bi + 1, 1 - slot)

            x = xbuf[slot].astype(BF16)
            hg = jnp.dot(x, wgb[...], preferred_element_type=F32)
            hu = jnp.dot(x, wub[...], preferred_element_type=F32)
            hd = (hg * jax.nn.sigmoid(hg) * hu).astype(BF16)
            y = jnp.dot(hd, wdb[...], preferred_element_type=F32)

            @pl.when(bi > 0)
            def _():
                scatter_wait(bi - 1)

            ybuf[...] = y
            scatter_start(bi)
            return carry

        lax.fori_loop(0, nblk, block, 0)
        scatter_wait(nblk - 1)


def moe_experts(hn_p, hn_s, eid, w_gate, w_up, w_down):
    n_prompt, d = hn_p.shape
    n_sample = hn_s.shape[0]
    n_rows = eid.shape[0] * 2
    flat_e = eid.reshape(n_rows)
    onehot = (flat_e[:, None] == jnp.arange(N_EXPERTS, dtype=I32)[None, :]).astype(I32)
    counts = jnp.sum(onehot, axis=0)
    pcounts = jnp.sum(onehot[:2 * n_prompt], axis=0)
    starts = jnp.cumsum(counts) - counts
    rank = jnp.sum(onehot * (jnp.cumsum(onehot, axis=0) - 1), axis=1)
    pos = starts[flat_e] + rank
    src = jnp.zeros((n_rows,), I32).at[pos].set(jnp.arange(n_rows, dtype=I32))
    any_spec = pl.BlockSpec(memory_space=pl.ANY)
    grid_spec = pltpu.PrefetchScalarGridSpec(
        num_scalar_prefetch=4,
        grid=(N_EXPERTS,),
        in_specs=[any_spec, any_spec,
                  pl.BlockSpec((1, d, D_FF), lambda e, *_: (e, 0, 0)),
                  pl.BlockSpec((1, d, D_FF), lambda e, *_: (e, 0, 0)),
                  pl.BlockSpec((1, D_FF, d), lambda e, *_: (e, 0, 0))],
        out_specs=[any_spec, any_spec],
        scratch_shapes=[pltpu.VMEM((d, D_FF), BF16), pltpu.VMEM((d, D_FF), BF16), pltpu.VMEM((D_FF, d), BF16),
                        pltpu.VMEM((2, MOE_BLK, d), F32), pltpu.VMEM((MOE_BLK, d), F32),
                        pltpu.SemaphoreType.DMA((2,)), pltpu.SemaphoreType.DMA(())],
    )
    return pl.pallas_call(
        functools.partial(_moe_kernel, n_prompt=n_prompt, n_sample=n_sample),
        grid_spec=grid_spec,
        out_shape=[jax.ShapeDtypeStruct((2 * n_prompt, d), F32), jax.ShapeDtypeStruct((2 * n_sample, d), F32)],
        compiler_params=_cparams(("arbitrary",), 56),
        name="moe_experts",
    )(starts, counts, pcounts, src, hn_p, hn_s, w_gate, w_up, w_down)


def _final_kernel(h_ref, y0_ref, y1_ref, ew_ref, nw_ref, o_ref):
    ew = ew_ref[...]
    x = h_ref[...] + ew[:, 0:1] * y0_ref[...] + ew[:, 1:2] * y1_ref[...]
    ms = jnp.mean(x * x, axis=-1, keepdims=True)
    o_ref[...] = x * lax.rsqrt(ms + NORM_EPS) * nw_ref[...]


def combine_final(h, y_rows, ew, norm_w):
    m, d = h.shape
    tm = min(m, ROW_TILE)
    o1 = m // tm
    return pl.pallas_call(
        _final_kernel,
        grid=(m // tm,),
        in_specs=[pl.BlockSpec((tm, d), lambda i: (i, 0)),
                  pl.BlockSpec((tm, d), lambda i: (i, 0)),
                  pl.BlockSpec((tm, d), lambda i: (i + o1, 0)),
                  pl.BlockSpec((tm, 2), lambda i: (i, 0)), _full((1, d))],
        out_specs=pl.BlockSpec((tm, d), lambda i: (i, 0)),
        out_shape=jax.ShapeDtypeStruct((m, d), F32),
        compiler_params=_cparams(("parallel",), 48),
        name="combine_final",
    )(h, y_rows, y_rows, ew, norm_w.reshape(1, d))


_RW_COL0 = 3 * ATT_W
_GATE_COL0 = _RW_COL0 + RW_PROJ


def _in_proj(x2, prm, q_dtype):
    xn = rmsnorm_bf16(x2, prm["norm_mix_w"])
    w_in = prm["w_in"]
    q = matmul_cols(xn, w_in, 0, ATT_W, q_dtype)
    k = matmul_cols(xn, w_in, ATT_W, ATT_W, F32)
    v = matmul_cols(xn, w_in, 2 * ATT_W, ATT_W, F32)
    prw = matmul_cols(xn, w_in, _RW_COL0, RW_PROJ, F32)
    g_lin = matmul_cols(xn, w_in, _GATE_COL0, 2 * D_MODEL, F32)
    return q, k, v, prw, g_lin


def _mixer_tail(x2, y_att, y_rw, g_lin, prm):
    merged = merge_branches(y_att, y_rw, g_lin, prm["b_gate"], prm["wa_bf16"], prm["wb_bf16"])
    return outproj_router(merged, prm["wo_bf16"], x2, prm["norm_ffn_w"], prm["w_router"], prm["b_router"])


def kernel(x_prompt, x_sample, cache_k, cache_v, state_wkv, state_shift, page_table, norm_mix_w, w_in, b_gate, lambda_q1, lambda_k1, lambda_q2, lambda_k2, subln_w, rel_bias, rw_mu, rw_w0, rw_w2, rw_a0, rw_a2, rw_g2, rw_k_k, rw_k_a, rw_r_k, rw_ln_w, rw_ln_b, w_branch_a, w_branch_b, w_out, norm_ffn_w, w_router_g, b_router_g, w_router_e, b_router_e, w_gate_e, w_up_e, w_down_e, norm_final_w):
    assert w_in.shape[0] == 1, "single-layer trunk"
    bp, sp, d = x_prompt.shape
    bs, ss, _ = x_sample.shape
    assert ss == 1 and d == D_MODEL
    n_pool = cache_k.shape[1]
    pad_lanes = LANES - N_GROUPS - N_EXPERTS
    prm = {
        "norm_mix_w": norm_mix_w[0], "w_in": w_in[0], "b_gate": b_gate[0],
        "rw_mu": rw_mu[0], "rw_w0": rw_w0[0], "rw_w2": rw_w2[0], "rw_a0": rw_a0[0], "rw_a2": rw_a2[0],
        "rw_g2": rw_g2[0], "rw_k_k": rw_k_k[0], "rw_k_a": rw_k_a[0], "rw_r_k": rw_r_k[0].reshape(RW_W),
        "rw_ln_w": rw_ln_w[0], "rw_ln_b": rw_ln_b[0],
        "wa_bf16": w_branch_a[0].astype(BF16), "wb_bf16": w_branch_b[0].astype(BF16),
        "wo_bf16": w_out[0].astype(BF16), "norm_ffn_w": norm_ffn_w[0],
        "w_router": jnp.concatenate([w_router_g[0], w_router_e[0], jnp.zeros((d, pad_lanes), F32)],
                                    axis=1).astype(BF16),
        "b_router": jnp.concatenate([b_router_g[0], b_router_e[0], jnp.zeros((pad_lanes,), F32)]).reshape(1, LANES),
    }
    lamv = jnp.stack([lambda_q1[0], lambda_k1[0], lambda_q2[0], lambda_k2[0]]).astype(F32)
    sub_w = subln_w[0]

    mp = bp * sp
    xp2 = x_prompt.reshape(mp, d)
    q_p, k_p, v_p, prw_p, g_p = _in_proj(xp2, prm, BF16)
    att_p = attn_prompt(q_p, k_p, v_p, lamv, rel_bias, sub_w, bp, sp)
    rw_p, wkv_p = rwkv_mix(prw_p, jnp.zeros((bp, RW_PROJ), F32), jnp.zeros((bp, RW_HEADS, RW_N, RW_N), F32),
                           prm, bp, sp)
    h_p, hn_p, eid_p, ew_p = _mixer_tail(xp2, att_p, rw_p, g_p, prm)

    xs2 = x_sample.reshape(bs, d)
    q_s, k_s, v_s, prw_s, g_s = _in_proj(xs2, prm, F32)
    heads = lambda a: a.reshape(bs, ATT_HEADS, HEAD_W)
    att_s = attn_decode(heads(q_s), heads(k_s), heads(v_s),
                        cache_k[0].reshape(n_pool, PAGE * ATT_HEADS, HEAD_W),
                        cache_v[0].reshape(n_pool, PAGE * ATT_HEADS, HEAD_W),
                        page_table, lamv, rel_bias, sub_w).reshape(bs, ATT_W)
    rw_s, wkv_s = rwkv_mix(prw_s, state_shift[0], state_wkv[0], prm, bs, 1)
    h_s, hn_s, eid_s, ew_s = _mixer_tail(xs2, att_s, rw_s, g_s, prm)

    yr_p, yr_s = moe_experts(hn_p, hn_s, jnp.concatenate([eid_p, eid_s], axis=0),
                             w_gate_e[0], w_up_e[0], w_down_e[0])
    y_p = combine_final(h_p, yr_p, ew_p, norm_final_w)
    y_s = combine_final(h_s, yr_s, ew_s, norm_final_w)

    return (y_p.reshape(bp, sp, d), y_s.reshape(bs, 1, d),
            k_p.reshape(1, bp, sp, ATT_HEADS, HEAD_W), v_p.reshape(1, bp, sp, ATT_HEADS, HEAD_W),
            wkv_p.astype(state_wkv.dtype)[None], prw_p.reshape(bp, sp, RW_PROJ)[None, :, -1],
            k_s.reshape(1, bs, 1, ATT_HEADS, HEAD_W), v_s.reshape(1, bs, 1, ATT_HEADS, HEAD_W),
            wkv_s.astype(state_wkv.dtype)[None], prw_s.reshape(1, bs, RW_PROJ))
```

```python
import functools
import math

import jax
import jax.numpy as jnp
from jax import lax
from jax.experimental import pallas as pl
from jax.experimental.pallas import tpu as pltpu

F32 = jnp.float32
BF16 = jnp.bfloat16
I32 = jnp.int32

D_MODEL = 2048
ATT_HEADS = 8
ATT_DK = 64
HEAD_W = 2 * ATT_DK
ATT_W = ATT_HEADS * HEAD_W
ATT_SCALE = ATT_DK ** -0.5
RW_HEADS = 16
RW_N = 64
RW_W = RW_HEADS * RW_N
RW_LORA_W = 128
RW_GATE_W = 128
RW_PROJ = 3 * RW_W + RW_LORA_W + RW_GATE_W
N_BUCKETS = 32
MAX_DISTANCE = 128
PAGE = 128
EXPERTS_PER_GROUP = 8
N_GROUPS = 8
N_EXPERTS = N_GROUPS * EXPERTS_PER_GROUP
D_FF = 512
NORM_EPS = 1e-6
SUBLN_EPS = 1e-5
RW_LN_EPS = 64e-5
LAM_INIT = 0.8 - 0.6 * math.exp(-0.3 * 0)
NEG = -1e30
LANES = 128
MXU_DIM = 256
MOE_BLK = 128
MOE_UNROLL = 16
ROW_TILE = 256
MIB = 1024 * 1024


def _cparams(sem, vmem_mib):
    return pltpu.CompilerParams(dimension_semantics=sem, vmem_limit_bytes=vmem_mib * MIB)


def _full(shape):
    nd = len(shape)
    return pl.BlockSpec(shape, lambda *_: (0,) * nd)


def _rmsnorm_bf16_kernel(x_ref, w_ref, o_ref):
    x = x_ref[...]
    ms = jnp.mean(x * x, axis=-1, keepdims=True)
    o_ref[...] = (x * lax.rsqrt(ms + NORM_EPS) * w_ref[...]).astype(BF16)


def rmsnorm_bf16(x, w):
    m, d = x.shape
    tm = min(m, 512)
    return pl.pallas_call(
        _rmsnorm_bf16_kernel,
        grid=(m // tm,),
        in_specs=[pl.BlockSpec((tm, d), lambda i: (i, 0)), _full((1, d))],
        out_specs=pl.BlockSpec((tm, d), lambda i: (i, 0)),
        out_shape=jax.ShapeDtypeStruct((m, d), BF16),
        compiler_params=_cparams(("parallel",), 32),
        name="rmsnorm_bf16",
    )(x, w.reshape(1, d))


def _matmul_kernel(x_ref, w_ref, o_ref):
    o_ref[...] = jnp.dot(x_ref[...], w_ref[...].astype(BF16),
                         preferred_element_type=F32).astype(o_ref.dtype)


def matmul_cols(x, w, col0, ncols, out_dtype):
    m, k = x.shape
    tm = next(t for t in (2048, 1024, 512, 256, m) if m % t == 0)
    tn = next(t for t in (512, 256, 128) if col0 % t == 0 and ncols % t == 0)
    off = col0 // tn
    return pl.pallas_call(
        _matmul_kernel,
        grid=(m // tm, ncols // tn),
        in_specs=[pl.BlockSpec((tm, k), lambda i, j: (i, 0)),
                  pl.BlockSpec((k, tn), lambda i, j: (0, j + off))],
        out_specs=pl.BlockSpec((tm, tn), lambda i, j: (i, j)),
        out_shape=jax.ShapeDtypeStruct((m, ncols), out_dtype),
        compiler_params=_cparams(("parallel", "arbitrary"), 48),
        name="in_proj",
    )(x, w)


def _rel_bucket(dist):
    n = jnp.maximum(dist, 0)
    max_exact = N_BUCKETS // 2
    large = max_exact + (jnp.log(jnp.maximum(n, 1).astype(F32) / max_exact)
                         / math.log(MAX_DISTANCE / max_exact) * (N_BUCKETS - max_exact)).astype(I32)
    large = jnp.minimum(large, N_BUCKETS - 1)
    return jnp.where(n < max_exact, n, large)


def _bias_by_distance(rel_bias, n):
    d = jnp.arange(n, dtype=I32)
    table = rel_bias.astype(F32)[_rel_bucket(d)]
    far = rel_bias.astype(F32)[N_BUCKETS - 1]
    return (table - far[None, :]).T


def _toeplitz(vals, t):
    h = vals.shape[0]
    w = jnp.concatenate([vals[:, ::-1], jnp.zeros((h, 1), vals.dtype)], axis=1)
    flat = jnp.tile(w, (1, t))[:, :t * (2 * t - 1)]
    return flat.reshape(h, t, 2 * t - 1)[:, :, t - 1:]


def _lambda(lamv_ref):
    lv = lamv_ref[...]
    s1 = jnp.sum(lv[0:1] * lv[1:2], axis=-1, keepdims=True)
    s2 = jnp.sum(lv[2:3] * lv[3:4], axis=-1, keepdims=True)
    return jnp.exp(s1) - jnp.exp(s2) + LAM_INIT


def _split_halves(q):
    lane = lax.broadcasted_iota(I32, q.shape, 1)
    return jnp.concatenate([jnp.where(lane < ATT_DK, q, 0.0), jnp.where(lane >= ATT_DK, q, 0.0)], axis=0)


def _attn_prompt_kernel(lamv_ref, q_ref, k_ref, v_ref, bd_ref, bp_ref, sw_ref, o_ref, kb_ref, v1_ref, *, t, nq):
    kb_ref[...] = k_ref[...].astype(BF16)
    v1_ref[:, :HEAD_W] = v_ref[...].astype(BF16)
    v1_ref[:, HEAD_W:] = jnp.ones((v1_ref.shape[0], HEAD_W), BF16)
    lam = _lambda(lamv_ref)
    sw = sw_ref[...] * (1.0 - LAM_INIT)
    bd = bd_ref[0]
    bp = bp_ref[0]
    bd2 = jnp.concatenate([bd, bd], axis=0)
    bp2 = jnp.concatenate([bp, bp], axis=0)

    for i in range(nq):
        n_keys = (i + 1) * t
        q = q_ref[i * t:(i + 1) * t, :].astype(F32) * ATT_SCALE
        q2 = _split_halves(q).astype(BF16)
        s = lax.dot_general(q2, kb_ref[:n_keys, :], (((1,), (1,)), ((), ())), preferred_element_type=F32)
        near = [s[:, i * t:] + bd2]
        if i >= 1:
            near.insert(0, s[:, (i - 1) * t:i * t] + bp2)
        s = jnp.concatenate(([s[:, :(i - 1) * t]] if i >= 2 else []) + near, axis=1)
        e = jnp.exp(s - jnp.max(s, axis=-1, keepdims=True)).astype(BF16)
        a = jnp.dot(e, v1_ref[:n_keys, :], preferred_element_type=F32)
        a = a[:, :HEAD_W] / a[:, HEAD_W:]
        o = a[:t] - lam * a[t:]
        ms = jnp.mean(o * o, axis=-1, keepdims=True)
        o_ref[i * t:(i + 1) * t, :] = o * lax.rsqrt(ms + SUBLN_EPS) * sw


def attn_prompt(q, k, v, lamv, rel_bias, subln_w, batch, seq):
    t = min(seq, 256)
    nq = seq // t
    assert t >= LANES, "blocks two or more away must lie beyond the last distinct distance bucket"
    bd0 = _bias_by_distance(rel_bias, 2 * t)
    bdiag = _toeplitz(jnp.concatenate([jnp.full((ATT_HEADS, t - 1), NEG, F32), bd0[:, :t]], axis=1), t)
    bprev = _toeplitz(bd0[:, 1:], t)
    blk = lambda: pl.BlockSpec((seq, HEAD_W), lambda b, h: (b, h))
    return pl.pallas_call(
        functools.partial(_attn_prompt_kernel, t=t, nq=nq),
        grid=(batch, ATT_HEADS),
        in_specs=[_full((4, ATT_DK)), blk(), blk(), blk(),
                  pl.BlockSpec((1, t, t), lambda b, h: (h, 0, 0)),
                  pl.BlockSpec((1, t, t), lambda b, h: (h, 0, 0)),
                  _full((1, HEAD_W))],
        out_specs=blk(),
        out_shape=jax.ShapeDtypeStruct((batch * seq, ATT_W), F32),
        scratch_shapes=[pltpu.VMEM((seq, HEAD_W), BF16), pltpu.VMEM((seq, 2 * HEAD_W), BF16)],
        compiler_params=_cparams(("parallel", "parallel"), 48),
        name="attn_prompt",
    )(lamv, q, k, v, bdiag, bprev, subln_w.reshape(1, HEAD_W))


def _attn_decode_kernel(pt_ref, lamv_ref, q_ref, kn_ref, vn_ref, bias_ref, bnew_ref, sw_ref, *rest, pages):
    del pt_ref
    k_refs = rest[:pages]
    v_refs = rest[pages:2 * pages]
    o_ref, m_ref, l_ref, acc_ref = rest[2 * pages:]
    c = pl.program_id(1)
    last = pl.num_programs(1) - 1

    @pl.when(c == 0)
    def _():
        m_ref[...] = jnp.full(m_ref.shape, NEG, F32)
        l_ref[...] = jnp.zeros(l_ref.shape, F32)
        acc_ref[...] = jnp.zeros(acc_ref.shape, F32)

    q16 = _split_halves(q_ref[0] * ATT_SCALE)
    row_head = lax.broadcasted_iota(I32, (2 * ATT_HEADS, 1), 0) & (ATT_HEADS - 1)
    qh = [jnp.where(row_head == h, q16, 0.0).astype(BF16) for h in range(ATT_HEADS)]
    pairs = pages // 2
    st = {"s": [None] * pairs, "pv": jnp.zeros((2 * ATT_HEADS, HEAD_W), F32)}

    def head_rows(ref, h):
        return ref[0, pl.ds(h, PAGE, stride=ATT_HEADS), :]

    def two_pages(refs, pp, h):
        return jnp.concatenate([head_rows(refs[2 * pp], h), head_rows(refs[2 * pp + 1], h)], axis=0).astype(BF16)

    def logits(pp):
        sp = None
        for h in range(ATT_HEADS):
            d = lax.dot_general(qh[h], two_pages(k_refs, pp, h), (((1,), (1,)), ((), ())),
                                preferred_element_type=F32)
            sp = d if sp is None else sp + d
        st["s"][pp] = sp

    def softmax_update():
        s = jnp.concatenate(st["s"], axis=1)
        s = s + bias_ref[...] * (c == last).astype(F32)
        m = m_ref[...]
        mn = jnp.maximum(m, jnp.max(s, axis=-1, keepdims=True))
        st["alpha"] = jnp.exp(m - mn)
        e = jnp.exp(s - mn)
        l_ref[...] = st["alpha"] * l_ref[...] + jnp.sum(e, axis=-1, keepdims=True)
        m_ref[...] = mn
        st["e"] = e.astype(BF16)

    def weighted_values(pp):
        ep = st["e"][:, pp * 2 * PAGE:(pp + 1) * 2 * PAGE]
        for h in range(ATT_HEADS):
            r = jnp.dot(ep, two_pages(v_refs, pp, h), preferred_element_type=F32)
            st["pv"] = st["pv"] + jnp.where(row_head == h, r, 0.0)

    def accumulate():
        acc_ref[...] = st["alpha"] * acc_ref[...] + st["pv"]

    for pp in range(pairs):
        logits(pp)
    softmax_update()
    for pp in range(pairs):
        weighted_values(pp)
    accumulate()

    @pl.when(c == last)
    def _():
        lam = _lambda(lamv_ref)
        m0 = m_ref[...]
        k16 = jnp.concatenate([kn_ref[0], kn_ref[0]], axis=0)
        v16 = jnp.concatenate([vn_ref[0], vn_ref[0]], axis=0)
        s_new = jnp.sum(q16 * k16, axis=-1, keepdims=True) + bnew_ref[...]
        m1 = jnp.maximum(m0, s_new)
        a1 = jnp.exp(m0 - m1)
        e_new = jnp.exp(s_new - m1)
        l1 = a1 * l_ref[...] + e_new
        acc1 = a1 * acc_ref[...] + e_new * v16
        o = acc1[:ATT_HEADS] / l1[:ATT_HEADS] - lam * (acc1[ATT_HEADS:] / l1[ATT_HEADS:])
        ms = jnp.mean(o * o, axis=-1, keepdims=True)
        o_ref[0] = o * lax.rsqrt(ms + SUBLN_EPS) * (sw_ref[...] * (1.0 - LAM_INIT))


def attn_decode(q, k_new, v_new, cache_k, cache_v, page_table, lamv, rel_bias, subln_w):
    nb, n_pages = page_table.shape
    pages = next(g for g in (8, 4, 2) if n_pages % g == 0)
    n_chunks = n_pages // pages
    width = pages * PAGE
    bd0 = _bias_by_distance(rel_bias, width + 1)
    near = bd0[:, width - jnp.arange(width, dtype=I32)]
    bias_last = jnp.concatenate([near, near], axis=0)
    bnew = jnp.concatenate([bd0[:, 0:1], bd0[:, 0:1]], axis=0)
    row = lambda: pl.BlockSpec((1, ATT_HEADS, HEAD_W), lambda b, c, pt: (b, 0, 0))
    page = lambda g: pl.BlockSpec((1, PAGE * ATT_HEADS, HEAD_W), lambda b, c, pt: (pt[b, c * pages + g], 0, 0))
    const = lambda shape: pl.BlockSpec(shape, lambda b, c, pt: (0,) * len(shape))
    grid_spec = pltpu.PrefetchScalarGridSpec(
        num_scalar_prefetch=1,
        grid=(nb, n_chunks),
        in_specs=[const((4, ATT_DK)), row(), row(), row(), const((2 * ATT_HEADS, width)),
                  const((2 * ATT_HEADS, 1)), const((1, HEAD_W))]
                 + [page(g) for g in range(pages)] + [page(g) for g in range(pages)],
        out_specs=row(),
        scratch_shapes=[pltpu.VMEM((2 * ATT_HEADS, 1), F32), pltpu.VMEM((2 * ATT_HEADS, 1), F32),
                        pltpu.VMEM((2 * ATT_HEADS, HEAD_W), F32)],
    )
    return pl.pallas_call(
        functools.partial(_attn_decode_kernel, pages=pages),
        grid_spec=grid_spec,
        out_shape=jax.ShapeDtypeStruct((nb, ATT_HEADS, HEAD_W), F32),
        compiler_params=_cparams(("parallel", "arbitrary"), 48),
        name="attn_decode",
    )(page_table, lamv, q, k_new, v_new, bias_last, bnew, subln_w.reshape(1, HEAD_W),
      *([cache_k] * pages), *([cache_v] * pages))


def _stack_lane_blocks(x, width=LANES):
    return jnp.concatenate([x[:, c * width:(c + 1) * width] for c in range(x.shape[1] // width)], axis=0)


def _unstack_lane_blocks(y, n):
    return jnp.concatenate([y[c * n:(c + 1) * n] for c in range(y.shape[0] // n)], axis=1)


def _segsum(x, r2_ref):
    n = x.shape[0]
    xs = _stack_lane_blocks(x)
    hi = xs.astype(BF16)
    lo = (xs - hi.astype(F32)).astype(BF16)
    y = jnp.dot(jnp.concatenate([hi, lo], axis=1), r2_ref[...], preferred_element_type=F32)
    return _unstack_lane_blocks(y, n)


def _rwkv_prep_kernel(p_ref, sp_ref, mu_ref, w0_ref, w2_ref, a0_ref, a2_ref, g2_ref, kk_ref, ka_ref, rk_ref, r2_ref,
                      a_out, c_out, w_out, b_out, k_out, v_out, vkr_out, bon_out, g_out, prev_ref, *, tt):
    ti = pl.program_id(1)

    @pl.when(ti == 0)
    def _():
        prev_ref[...] = sp_ref[0]

    p = p_ref[0]
    prev_row = prev_ref[...]
    if tt > 1:
        rolled = pltpu.roll(p, 1, axis=0)
        row = lax.broadcasted_iota(I32, p.shape, 0)
        p_prev = jnp.where(row == 0, prev_row, rolled)
    else:
        p_prev = prev_row
    prev_ref[...] = p[tt - 1:tt, :]
    pm = p + mu_ref[...] * (p_prev - p)
    r = pm[:, 0:RW_W]
    kx = pm[:, RW_W:2 * RW_W]
    vx = pm[:, 2 * RW_W:3 * RW_W]
    lora = pm[:, 3 * RW_W:3 * RW_W + RW_LORA_W]
    gl = pm[:, 3 * RW_W + RW_LORA_W:RW_PROJ]
    dw = jnp.dot(jnp.tanh(lora).astype(BF16), w2_ref[...], preferred_element_type=F32)
    w = -jax.nn.softplus(-(w0_ref[...] + dw)) - 0.5
    decay = jnp.exp(-jnp.exp(w))
    a = jax.nn.sigmoid(a0_ref[...] + jnp.dot(lora.astype(BF16), a2_ref[...], preferred_element_type=F32))
    g = jnp.dot(jax.nn.sigmoid(gl).astype(BF16), g2_ref[...], preferred_element_type=F32)
    kk = kx * kk_ref[...]
    kkn = kk / jnp.maximum(jnp.sqrt(_segsum(kk * kk, r2_ref)), 1e-12)
    k = kx * (1.0 + (a - 1.0) * ka_ref[...])
    av = -kkn
    bv = kkn * a
    br = _segsum(bv * r, r2_ref)
    kr = _segsum(k * r, r2_ref)
    a_out[0] = av
    c_out[0] = decay * r + av * br
    w_out[0] = decay
    b_out[0] = bv
    k_out[0] = k
    v_out[0] = vx
    vkr_out[0] = vx * kr
    bon_out[0] = _segsum(r * k * rk_ref[...], r2_ref) * vx
    g_out[0] = g


def _value_diag():
    row = lax.broadcasted_iota(I32, (RW_N, RW_W), 0)
    lane = lax.broadcasted_iota(I32, (RW_N, RW_W), 1)
    return (lane & (RW_N - 1)) == row


def _scan_step(seq_refs, s_ref, t, e_j, zacc, rmat, diag, *, nb, v_terms):
    a_ref, c_ref, w_ref, b_ref, k_ref, v_ref = seq_refs
    width = rmat.shape[0]
    per = (RW_W // width) * RW_N
    row = lambda ref, bb: ref[bb, pl.ds(t, 1), :]
    stacked, pz = [], []
    for bb in range(nb):
        s = s_ref[bb]
        pa = (s * row(a_ref, bb)).astype(BF16)
        pz.append((s * row(c_ref, bb)).astype(BF16))
        v = row(v_ref, bb)
        v_hi = jnp.where(diag, v, 0.0).astype(BF16)
        terms = [pa, v_hi]
        if v_terms == 2:
            terms.append(jnp.where(diag, v - v_hi.astype(F32), 0.0).astype(BF16))
        stacked += [_stack_lane_blocks(x, width) for x in terms]
    red = jnp.dot(jnp.concatenate(stacked, axis=0), rmat, preferred_element_type=F32)
    nterm = 1 + v_terms
    for bb in range(nb):
        parts = [_unstack_lane_blocks(red[(nterm * bb + i) * per:(nterm * bb + i + 1) * per], RW_N)
                 for i in range(nterm)]
        vb = parts[1] if v_terms == 1 else parts[1] + parts[2]
        s_ref[bb] = s_ref[bb] * row(w_ref, bb) + parts[0] * row(b_ref, bb) + vb * row(k_ref, bb)
    return zacc + jnp.dot(jnp.concatenate(pz, axis=0), e_j, preferred_element_type=F32)


def _rwkv_scan_kernel(*refs, nb, tt, nst, v_terms):
    seq_refs = refs[:6]
    s0_ref, r_ref, e_ref, y_ref, so_ref, s_ref = refs[6:]
    ti = pl.program_id(1)

    @pl.when(ti == 0)
    def _():
        s_ref[...] = s0_ref[...]

    diag = _value_diag()
    rmat = r_ref[...]

    def group(g, carry):
        zacc = jnp.zeros((nb * RW_N, LANES), F32)
        for j in range(nst):
            zacc = _scan_step(seq_refs, s_ref, g * nst + j, e_ref[j], zacc, rmat, diag, nb=nb, v_terms=v_terms)
        for bb in range(nb):
            y_ref[bb, g] = zacc[bb * RW_N:(bb + 1) * RW_N]
        return carry

    lax.fori_loop(0, tt // nst, group, 0)

    @pl.when(ti == pl.num_programs(1) - 1)
    def _():
        so_ref[...] = s_ref[...]


def _rwkv_post_kernel(y_ref, vkr_ref, bon_ref, g_ref, lw_ref, lb_ref, r2_ref, o_ref):
    y = y_ref[...] + vkr_ref[...]
    mu = _segsum(y, r2_ref) * (1.0 / RW_N)
    d = y - mu
    var = _segsum(d * d, r2_ref) * (1.0 / RW_N)
    yn = d * lax.rsqrt(var + RW_LN_EPS) * lw_ref[...] + lb_ref[...]
    o_ref[...] = ((yn + bon_ref[...]) * g_ref[...]).astype(BF16)


def _segment_ones(width):
    return jnp.kron(jnp.eye(width // RW_N, dtype=F32), jnp.ones((RW_N, RW_N), F32)).astype(BF16)


def _segsum_operand():
    seg = _segment_ones(LANES)
    return jnp.concatenate([seg, seg], axis=0)


def rwkv_prep(prw, shift_prev, prm, batch, seq):
    tt = min(seq, 128)
    r2 = _segsum_operand()
    zeros64 = jnp.zeros((RW_N, RW_W), F32)
    w2p = jnp.concatenate([prm["rw_w2"], zeros64], axis=0).astype(BF16)
    a2p = jnp.concatenate([zeros64, prm["rw_a2"]], axis=0).astype(BF16)
    row = lambda a: a.reshape(1, -1)
    blk_in = pl.BlockSpec((1, tt, RW_PROJ), lambda b, i: (b, i, 0))
    blk_out = pl.BlockSpec((1, tt, RW_W), lambda b, i: (b, i, 0))
    c2 = lambda shape: pl.BlockSpec(shape, lambda b, i: (0,) * len(shape))
    return pl.pallas_call(
        functools.partial(_rwkv_prep_kernel, tt=tt),
        grid=(batch, seq // tt),
        in_specs=[blk_in, pl.BlockSpec((1, 1, RW_PROJ), lambda b, i: (b, 0, 0)), c2((1, RW_PROJ)),
                  c2((1, RW_W)), c2((RW_LORA_W, RW_W)), c2((1, RW_W)), c2((RW_LORA_W, RW_W)),
                  c2((RW_GATE_W, RW_W)), c2((1, RW_W)), c2((1, RW_W)), c2((1, RW_W)), c2((2 * LANES, LANES))],
        out_specs=[blk_out] * 9,
        out_shape=[jax.ShapeDtypeStruct((batch, seq, RW_W), F32)] * 9,
        scratch_shapes=[pltpu.VMEM((1, RW_PROJ), F32)],
        compiler_params=_cparams(("parallel", "arbitrary"), 56),
        name="rwkv_prep",
    )(prw.reshape(batch, seq, RW_PROJ), shift_prev.reshape(batch, 1, RW_PROJ), row(prm["rw_mu"]),
      row(prm["rw_w0"]), w2p, row(prm["rw_a0"]), a2p, prm["rw_g2"].astype(BF16), row(prm["rw_k_k"]),
      row(prm["rw_k_a"]), row(prm["rw_r_k"]), r2)


SCAN_NB = 4


def _scan_group(seq):
    return min(seq, LANES // RW_HEADS)


def _scan_operands(wkv0, batch, seq):
    nst = _scan_group(seq)
    head = jnp.arange(RW_W, dtype=I32) // RW_N
    ecomp = (head[None, :, None] + RW_HEADS * jnp.arange(nst, dtype=I32)[:, None, None]
             == jnp.arange(LANES, dtype=I32)[None, None, :]).astype(BF16)
    s0 = wkv0.astype(F32).transpose(0, 2, 1, 3).reshape(batch, RW_N, RW_W)
    return s0, _segment_ones(MXU_DIM), ecomp


def _scan_out_shapes(batch, seq):
    return [jax.ShapeDtypeStruct((batch, seq // _scan_group(seq), RW_N, LANES), F32),
            jax.ShapeDtypeStruct((batch, RW_N, RW_W), F32)]


def rwkv_scan(seq_arrays, wkv0, batch, seq):
    nst = _scan_group(seq)
    ts = min(seq, 64)
    nb = SCAN_NB
    v_terms = 2 if seq == 1 else 1
    c2 = lambda shape: pl.BlockSpec(shape, lambda b, i: (0,) * len(shape))
    seq_blk = pl.BlockSpec((nb, ts, RW_W), lambda b, i: (b, i, 0))
    st_blk = pl.BlockSpec((nb, RW_N, RW_W), lambda b, i: (b, 0, 0))
    return pl.pallas_call(
        functools.partial(_rwkv_scan_kernel, nb=nb, tt=ts, nst=nst, v_terms=v_terms),
        grid=(batch // nb, seq // ts),
        in_specs=[seq_blk] * 6 + [st_blk, c2((MXU_DIM, MXU_DIM)), c2((nst, RW_W, LANES))],
        out_specs=[pl.BlockSpec((nb, ts // nst, RW_N, LANES), lambda b, i: (b, i, 0, 0)), st_blk],
        out_shape=_scan_out_shapes(batch, seq),
        scratch_shapes=[pltpu.VMEM((nb, RW_N, RW_W), F32)],
        compiler_params=_cparams(("parallel", "arbitrary"), 48),
        name="rwkv_scan",
    )(*seq_arrays, *_scan_operands(wkv0, batch, seq))


def rwkv_finish(yc, s_fin, vkr, bon, gv, prm, batch, seq):
    m = batch * seq
    nst = _scan_group(seq)
    row = lambda a: a.reshape(1, -1)
    yz = yc.reshape(batch, seq // nst, RW_N, LANES // RW_HEADS, RW_HEADS)[:, :, :, :nst, :]
    yz = yz.transpose(0, 1, 3, 4, 2).reshape(m, RW_W)
    wkv_new = s_fin.reshape(batch, RW_N, RW_HEADS, RW_N).transpose(0, 2, 1, 3)

    tm = min(m, ROW_TILE)
    rows = pl.BlockSpec((tm, RW_W), lambda i: (i, 0))
    y = pl.pallas_call(
        _rwkv_post_kernel,
        grid=(m // tm,),
        in_specs=[rows] * 4 + [_full((1, RW_W)), _full((1, RW_W)), _full((2 * LANES, LANES))],
        out_specs=rows,
        out_shape=jax.ShapeDtypeStruct((m, RW_W), BF16),
        compiler_params=_cparams(("parallel",), 48),
        name="rwkv_post",
    )(yz, vkr.reshape(m, RW_W), bon.reshape(m, RW_W), gv.reshape(m, RW_W), row(prm["rw_ln_w"]),
      row(prm["rw_ln_b"]), _segsum_operand())
    return y, wkv_new


def rwkv_mix(prw, shift_prev, wkv0, prm, batch, seq):
    *seq_arrays, vkr, bon, gv = rwkv_prep(prw, shift_prev, prm, batch, seq)
    yc, s_fin = rwkv_scan(seq_arrays[:6], wkv0, batch, seq)
    return rwkv_finish(yc, s_fin, vkr, bon, gv, prm, batch, seq)


def _merge_kernel(ya_ref, yb_ref, ga_ref, gb_ref, ba_ref, bb_ref, wa_ref, wb_ref, o_ref):
    pa = jnp.dot(ya_ref[...].astype(BF16), wa_ref[...], preferred_element_type=F32)
    pb = jnp.dot(yb_ref[...], wb_ref[...], preferred_element_type=F32)
    ga = jax.nn.sigmoid(ga_ref[...] + ba_ref[...])
    gb = jax.nn.sigmoid(gb_ref[...] + bb_ref[...])
    o_ref[...] = (ga * pa + gb * pb).astype(BF16)


def merge_branches(y_att, y_rw, g_lin, b_gate, wa, wb):
    m = y_att.shape[0]
    tm = min(m, ROW_TILE)
    d = D_MODEL
    return pl.pallas_call(
        _merge_kernel,
        grid=(m // tm,),
        in_specs=[pl.BlockSpec((tm, ATT_W), lambda i: (i, 0)), pl.BlockSpec((tm, RW_W), lambda i: (i, 0)),
                  pl.BlockSpec((tm, d), lambda i: (i, 0)), pl.BlockSpec((tm, d), lambda i: (i, 1)),
                  pl.BlockSpec((1, d), lambda i: (0, 0)), pl.BlockSpec((1, d), lambda i: (0, 1)),
                  _full((ATT_W, d)), _full((RW_W, d))],
        out_specs=pl.BlockSpec((tm, d), lambda i: (i, 0)),
        out_shape=jax.ShapeDtypeStruct((m, d), BF16),
        compiler_params=_cparams(("parallel",), 48),
        name="merge_branches",
    )(y_att, y_rw, g_lin, g_lin, b_gate.reshape(1, 2 * d), b_gate.reshape(1, 2 * d), wa, wb)


SLAB = D_MODEL // LANES


def _store_token_slabs(ref, x):
    n = x.shape[0]
    for c in range(SLAB):
        ref[pl.ds(c, n, stride=SLAB), :] = x[:, c * LANES:(c + 1) * LANES]


def _load_token_slabs(ref, n):
    return jnp.concatenate([ref[pl.ds(c, n, stride=SLAB), :] for c in range(SLAB)], axis=1)


def _outproj_router_kernel(mg_ref, wo_ref, x_ref, nw_ref, wr_ref, br_ref, h_ref, hn_ref, eid_ref, ew_ref):
    h = x_ref[...] + jnp.dot(mg_ref[...], wo_ref[...], preferred_element_type=F32)
    h_ref[...] = h
    ms = jnp.mean(h * h, axis=-1, keepdims=True)
    hn = h * lax.rsqrt(ms + NORM_EPS) * nw_ref[...]
    _store_token_slabs(hn_ref, hn)
    lg = jnp.dot(hn.astype(BF16), wr_ref[...], preferred_element_type=F32) + br_ref[...]
    lane = lax.broadcasted_iota(I32, lg.shape, 1)
    lane_f = lane.astype(F32)
    big = float(LANES)
    lgm = jnp.where(lane < N_GROUPS, lg, NEG)
    gmax = jnp.max(lgm, axis=-1, keepdims=True)
    gidx = jnp.min(jnp.where(lgm == gmax, lane_f, big), axis=-1, keepdims=True)
    g_w = 1.0 / jnp.sum(jnp.exp(lgm - gmax), axis=-1, keepdims=True)
    lane_group = ((lane >> 3) - 1).astype(F32)
    in_group = (lane >= N_GROUPS) & (lane < N_GROUPS + N_EXPERTS) & (lane_group == gidx)
    le = jnp.where(in_group, lg, NEG)
    m1 = jnp.max(le, axis=-1, keepdims=True)
    i1 = jnp.min(jnp.where(le == m1, lane_f, big), axis=-1, keepdims=True)
    le2 = jnp.where(lane_f == i1, NEG, le)
    m2 = jnp.max(le2, axis=-1, keepdims=True)
    i2 = jnp.min(jnp.where(le2 == m2, lane_f, big), axis=-1, keepdims=True)
    e2 = jnp.exp(m2 - m1)
    w1 = g_w / (1.0 + e2)
    w2 = g_w * e2 / (1.0 + e2)
    col = lax.broadcasted_iota(I32, eid_ref.shape, 1)
    eid_ref[...] = jnp.where(col == 0, i1, i2).astype(I32) - N_GROUPS
    ew_ref[...] = jnp.where(col == 0, w1, w2)


def outproj_router(merged, w_out, x, norm_w, w_router, b_router):
    m, d = x.shape
    tm = min(m, ROW_TILE)
    rows = lambda w: pl.BlockSpec((tm, w), lambda i: (i, 0))
    return pl.pallas_call(
        _outproj_router_kernel,
        grid=(m // tm,),
        in_specs=[rows(d), _full((d, d)), rows(d), _full((1, d)), _full((d, LANES)), _full((1, LANES))],
        out_specs=[rows(d), pl.BlockSpec((tm * SLAB, LANES), lambda i: (i, 0)), rows(2), rows(2)],
        out_shape=[jax.ShapeDtypeStruct((m, d), F32), jax.ShapeDtypeStruct((m * SLAB, LANES), F32),
                   jax.ShapeDtypeStruct((m, 2), I32), jax.ShapeDtypeStruct((m, 2), F32)],
        compiler_params=_cparams(("parallel",), 48),
        name="outproj_router",
    )(merged, w_out, x, norm_w.reshape(1, d), w_router, b_router)


def _moe_kernel(starts_ref, counts_ref, pcounts_ref, src_ref, hp_ref, hs_ref, wg_ref, wu_ref, wd_ref,
                yp_ref, ys_ref, wgb, wub, wdb, xbuf, ybuf, sem_in, sem_out, *, n_prompt, n_sample):
    e = pl.program_id(0)

    @pl.when(e == 0)
    def _():
        xbuf[...] = jnp.zeros(xbuf.shape, F32)

    n = counts_ref[e]
    n_p = pcounts_ref[e]
    start = starts_ref[e]
    nblk = (n + MOE_BLK - 1) // MOE_BLK

    def slab(i):
        return pl.ds(pl.multiple_of(i * SLAB, SLAB), SLAB)

    def row_in(src_hbm, tok, slot, r):
        return pltpu.make_async_copy(src_hbm.at[slab(tok)], xbuf.at[slot, slab(r)], sem_in.at[slot])

    def row_out(dst_hbm, dst, r):
        return pltpu.make_async_copy(ybuf.at[slab(r)], dst_hbm.at[slab(dst)], sem_out)

    def for_rows(lo, hi, body):
        whole = (lo == 0) & (hi == MOE_BLK)

        @pl.when(whole)
        def _():
            def chunk(i, c):
                for u in range(MOE_UNROLL):
                    body(i * MOE_UNROLL + u)
                return c
            lax.fori_loop(0, MOE_BLK // MOE_UNROLL, chunk, 0)

        @pl.when(jnp.logical_not(whole))
        def _():
            lax.fori_loop(lo, hi, lambda r, c: (body(r), c)[1], 0)

    def block_rows(bi):
        nv = jnp.minimum(MOE_BLK, n - bi * MOE_BLK)
        nvp = jnp.clip(n_p - bi * MOE_BLK, 0, nv)
        return start + bi * MOE_BLK, nv, nvp

    def gather_start(bi, slot):
        r0, nv, nvp = block_rows(bi)
        for_rows(0, nvp, lambda r: row_in(hp_ref, src_ref[r0 + r] >> 1, slot, r).start())
        for_rows(nvp, nv, lambda r: row_in(hs_ref, (src_ref[r0 + r] >> 1) - n_prompt, slot, r).start())

    def gather_wait(bi, slot):
        _, nv, _ = block_rows(bi)
        for_rows(0, nv, lambda r: row_in(hp_ref, 0, slot, r).wait())

    def scatter_start(bi):
        r0, nv, nvp = block_rows(bi)

        def to_prompt(r):
            src = src_ref[r0 + r]
            row_out(yp_ref, (src & 1) * n_prompt + (src >> 1), r).start()

        def to_sample(r):
            src = src_ref[r0 + r]
            row_out(ys_ref, (src & 1) * n_sample + (src >> 1) - n_prompt, r).start()

        for_rows(0, nvp, to_prompt)
        for_rows(nvp, nv, to_sample)

    def scatter_wait(bi):
        _, nv, _ = block_rows(bi)
        for_rows(0, nv, lambda r: row_out(yp_ref, 0, r).wait())

    @pl.when(n > 0)
    def _():
        gather_start(0, 0)
        wgb[...] = wg_ref[0].astype(BF16)
        wub[...] = wu_ref[0].astype(BF16)
        wdb[...] = wd_ref[0].astype(BF16)

        def block(bi, carry):
            slot = bi & 1
            gather_wait(bi, slot)

            @pl.when(bi + 1 < nblk)
            def _():
                gather_start(bi + 1, 1 - slot)

            x = _load_token_slabs(xbuf.at[slot], MOE_BLK).astype(BF16)
            hg = jnp.dot(x, wgb[...], preferred_element_type=F32)
            hu = jnp.dot(x, wub[...], preferred_element_type=F32)
            hd = (hg * jax.nn.sigmoid(hg) * hu).astype(BF16)
            y = jnp.dot(hd, wdb[...], preferred_element_type=F32)

            @pl.when(bi > 0)
            def _():
                scatter_wait(bi - 1)

            _store_token_slabs(ybuf, y)
            scatter_start(bi)
            return carry

        lax.fori_loop(0, nblk, block, 0)
        scatter_wait(nblk - 1)


def moe_experts(hn_p, hn_s, eid, w_gate, w_up, w_down):
    d = D_MODEL
    n_prompt = hn_p.shape[0] // SLAB
    n_sample = hn_s.shape[0] // SLAB
    n_rows = eid.shape[0] * 2
    flat_e = eid.reshape(n_rows)
    onehot = (flat_e[:, None] == jnp.arange(N_EXPERTS, dtype=I32)[None, :]).astype(I32)
    counts = jnp.sum(onehot, axis=0)
    pcounts = jnp.sum(onehot[:2 * n_prompt], axis=0)
    starts = jnp.cumsum(counts) - counts
    rank = jnp.sum(onehot * (jnp.cumsum(onehot, axis=0) - 1), axis=1)
    pos = starts[flat_e] + rank
    src = jnp.zeros((n_rows,), I32).at[pos].set(jnp.arange(n_rows, dtype=I32))
    any_spec = pl.BlockSpec(memory_space=pl.ANY)
    grid_spec = pltpu.PrefetchScalarGridSpec(
        num_scalar_prefetch=4,
        grid=(N_EXPERTS,),
        in_specs=[any_spec, any_spec,
                  pl.BlockSpec((1, d, D_FF), lambda e, *_: (e, 0, 0)),
                  pl.BlockSpec((1, d, D_FF), lambda e, *_: (e, 0, 0)),
                  pl.BlockSpec((1, D_FF, d), lambda e, *_: (e, 0, 0))],
        out_specs=[any_spec, any_spec],
        scratch_shapes=[pltpu.VMEM((d, D_FF), BF16), pltpu.VMEM((d, D_FF), BF16), pltpu.VMEM((D_FF, d), BF16),
                        pltpu.VMEM((2, MOE_BLK * SLAB, LANES), F32), pltpu.VMEM((MOE_BLK * SLAB, LANES), F32),
                        pltpu.SemaphoreType.DMA((2,)), pltpu.SemaphoreType.DMA(())],
    )
    return pl.pallas_call(
        functools.partial(_moe_kernel, n_prompt=n_prompt, n_sample=n_sample),
        grid_spec=grid_spec,
        out_shape=[jax.ShapeDtypeStruct((2 * n_prompt * SLAB, LANES), F32),
                   jax.ShapeDtypeStruct((2 * n_sample * SLAB, LANES), F32)],
        compiler_params=_cparams(("arbitrary",), 56),
        name="moe_experts",
    )(starts, counts, pcounts, src, hn_p, hn_s, w_gate, w_up, w_down)


def _final_kernel(h_ref, y0_ref, y1_ref, ew_ref, nw_ref, o_ref):
    ew = ew_ref[...]
    tm = h_ref.shape[0]
    x = (h_ref[...] + ew[:, 0:1] * _load_token_slabs(y0_ref, tm) + ew[:, 1:2] * _load_token_slabs(y1_ref, tm))
    ms = jnp.mean(x * x, axis=-1, keepdims=True)
    o_ref[...] = x * lax.rsqrt(ms + NORM_EPS) * nw_ref[...]


def combine_final(h, y_rows, ew, norm_w):
    m, d = h.shape
    tm = min(m, ROW_TILE)
    o1 = m // tm
    return pl.pallas_call(
        _final_kernel,
        grid=(m // tm,),
        in_specs=[pl.BlockSpec((tm, d), lambda i: (i, 0)),
                  pl.BlockSpec((tm * SLAB, LANES), lambda i: (i, 0)),
                  pl.BlockSpec((tm * SLAB, LANES), lambda i: (i + o1, 0)),
                  pl.BlockSpec((tm, 2), lambda i: (i, 0)), _full((1, d))],
        out_specs=pl.BlockSpec((tm, d), lambda i: (i, 0)),
        out_shape=jax.ShapeDtypeStruct((m, d), F32),
        compiler_params=_cparams(("parallel",), 48),
        name="combine_final",
    )(h, y_rows, y_rows, ew, norm_w.reshape(1, d))


_RW_COL0 = 3 * ATT_W
_GATE_COL0 = _RW_COL0 + RW_PROJ


def _in_proj(x2, prm, q_dtype):
    xn = rmsnorm_bf16(x2, prm["norm_mix_w"])
    w_in = prm["w_in"]
    q = matmul_cols(xn, w_in, 0, ATT_W, q_dtype)
    k = matmul_cols(xn, w_in, ATT_W, ATT_W, F32)
    v = matmul_cols(xn, w_in, 2 * ATT_W, ATT_W, F32)
    prw = matmul_cols(xn, w_in, _RW_COL0, RW_PROJ, F32)
    g_lin = matmul_cols(xn, w_in, _GATE_COL0, 2 * D_MODEL, F32)
    return q, k, v, prw, g_lin


def _mixer_tail(x2, y_att, y_rw, g_lin, prm):
    merged = merge_branches(y_att, y_rw, g_lin, prm["b_gate"], prm["wa_bf16"], prm["wb_bf16"])
    return outproj_router(merged, prm["wo_bf16"], x2, prm["norm_ffn_w"], prm["w_router"], prm["b_router"])


def kernel(x_prompt, x_sample, cache_k, cache_v, state_wkv, state_shift, page_table, norm_mix_w, w_in, b_gate, lambda_q1, lambda_k1, lambda_q2, lambda_k2, subln_w, rel_bias, rw_mu, rw_w0, rw_w2, rw_a0, rw_a2, rw_g2, rw_k_k, rw_k_a, rw_r_k, rw_ln_w, rw_ln_b, w_branch_a, w_branch_b, w_out, norm_ffn_w, w_router_g, b_router_g, w_router_e, b_router_e, w_gate_e, w_up_e, w_down_e, norm_final_w):
    assert w_in.shape[0] == 1, "single-layer trunk"
    bp, sp, d = x_prompt.shape
    bs, ss, _ = x_sample.shape
    assert ss == 1 and d == D_MODEL
    n_pool = cache_k.shape[1]
    pad_lanes = LANES - N_GROUPS - N_EXPERTS
    prm = {
        "norm_mix_w": norm_mix_w[0], "w_in": w_in[0], "b_gate": b_gate[0],
        "rw_mu": rw_mu[0], "rw_w0": rw_w0[0], "rw_w2": rw_w2[0], "rw_a0": rw_a0[0], "rw_a2": rw_a2[0],
        "rw_g2": rw_g2[0], "rw_k_k": rw_k_k[0], "rw_k_a": rw_k_a[0], "rw_r_k": rw_r_k[0].reshape(RW_W),
        "rw_ln_w": rw_ln_w[0], "rw_ln_b": rw_ln_b[0],
        "wa_bf16": w_branch_a[0].astype(BF16), "wb_bf16": w_branch_b[0].astype(BF16),
        "wo_bf16": w_out[0].astype(BF16), "norm_ffn_w": norm_ffn_w[0],
        "w_router": jnp.concatenate([w_router_g[0], w_router_e[0], jnp.zeros((d, pad_lanes), F32)],
                                    axis=1).astype(BF16),
        "b_router": jnp.concatenate([b_router_g[0], b_router_e[0], jnp.zeros((pad_lanes,), F32)]).reshape(1, LANES),
    }
    lamv = jnp.stack([lambda_q1[0], lambda_k1[0], lambda_q2[0], lambda_k2[0]]).astype(F32)
    sub_w = subln_w[0]

    mp = bp * sp
    xp2 = x_prompt.reshape(mp, d)
    q_p, k_p, v_p, prw_p, g_p = _in_proj(xp2, prm, BF16)
    att_p = attn_prompt(q_p, k_p, v_p, lamv, rel_bias, sub_w, bp, sp)
    rw_p, wkv_p = rwkv_mix(prw_p, jnp.zeros((bp, RW_PROJ), F32), jnp.zeros((bp, RW_HEADS, RW_N, RW_N), F32),
                           prm, bp, sp)
    h_p, hn_p, eid_p, ew_p = _mixer_tail(xp2, att_p, rw_p, g_p, prm)

    xs2 = x_sample.reshape(bs, d)
    q_s, k_s, v_s, prw_s, g_s = _in_proj(xs2, prm, F32)
    heads = lambda a: a.reshape(bs, ATT_HEADS, HEAD_W)
    att_s = attn_decode(heads(q_s), heads(k_s), heads(v_s),
                        cache_k[0].reshape(n_pool, PAGE * ATT_HEADS, HEAD_W),
                        cache_v[0].reshape(n_pool, PAGE * ATT_HEADS, HEAD_W),
                        page_table, lamv, rel_bias, sub_w).reshape(bs, ATT_W)
    rw_s, wkv_s = rwkv_mix(prw_s, state_shift[0], state_wkv[0], prm, bs, 1)
    h_s, hn_s, eid_s, ew_s = _mixer_tail(xs2, att_s, rw_s, g_s, prm)

    yr_p, yr_s = moe_experts(hn_p, hn_s, jnp.concatenate([eid_p, eid_s], axis=0),
                             w_gate_e[0], w_up_e[0], w_down_e[0])
    y_p = combine_final(h_p, yr_p, ew_p, norm_final_w)
    y_s = combine_final(h_s, yr_s, ew_s, norm_final_w)

    return (y_p.reshape(bp, sp, d), y_s.reshape(bs, 1, d),
            k_p.reshape(1, bp, sp, ATT_HEADS, HEAD_W), v_p.reshape(1, bp, sp, ATT_HEADS, HEAD_W),
            wkv_p.astype(state_wkv.dtype)[None], prw_p.reshape(bp, sp, RW_PROJ)[None, :, -1],
            k_s.reshape(1, bs, 1, ATT_HEADS, HEAD_W), v_s.reshape(1, bs, 1, ATT_HEADS, HEAD_W),
            wkv_s.astype(state_wkv.dtype)[None], prw_s.reshape(1, bs, RW_PROJ))
```

```python
import functools
import math

import jax
import jax.numpy as jnp
from jax import lax
from jax.experimental import pallas as pl
from jax.experimental.pallas import tpu as pltpu

F32 = jnp.float32
BF16 = jnp.bfloat16
I32 = jnp.int32

D_MODEL = 2048
ATT_HEADS = 8
ATT_DK = 64
HEAD_W = 2 * ATT_DK
ATT_W = ATT_HEADS * HEAD_W
ATT_SCALE = ATT_DK ** -0.5
RW_HEADS = 16
RW_N = 64
RW_W = RW_HEADS * RW_N
RW_LORA_W = 128
RW_GATE_W = 128
RW_PROJ = 3 * RW_W + RW_LORA_W + RW_GATE_W
N_BUCKETS = 32
MAX_DISTANCE = 128
PAGE = 128
EXPERTS_PER_GROUP = 8
N_GROUPS = 8
N_EXPERTS = N_GROUPS * EXPERTS_PER_GROUP
D_FF = 512
NORM_EPS = 1e-6
SUBLN_EPS = 1e-5
RW_LN_EPS = 64e-5
LAM_INIT = 0.8 - 0.6 * math.exp(-0.3 * 0)
NEG = -1e30
LANES = 128
MXU_DIM = 256
MOE_BLK = 128
MOE_UNROLL = 16
WRITEBACK_DMA_PRIORITY = 1
ROW_TILE = 256
MIB = 1024 * 1024


def _cparams(sem, vmem_mib):
    return pltpu.CompilerParams(dimension_semantics=sem, vmem_limit_bytes=vmem_mib * MIB)


def _full(shape):
    nd = len(shape)
    return pl.BlockSpec(shape, lambda *_: (0,) * nd)


def _rmsnorm_bf16_kernel(x_ref, w_ref, o_ref):
    x = x_ref[...]
    ms = jnp.mean(x * x, axis=-1, keepdims=True)
    o_ref[...] = (x * lax.rsqrt(ms + NORM_EPS) * w_ref[...]).astype(BF16)


def rmsnorm_bf16(x, w):
    m, d = x.shape
    tm = min(m, 512)
    return pl.pallas_call(
        _rmsnorm_bf16_kernel,
        grid=(m // tm,),
        in_specs=[pl.BlockSpec((tm, d), lambda i: (i, 0)), _full((1, d))],
        out_specs=pl.BlockSpec((tm, d), lambda i: (i, 0)),
        out_shape=jax.ShapeDtypeStruct((m, d), BF16),
        compiler_params=_cparams(("parallel",), 32),
        name="rmsnorm_bf16",
    )(x, w.reshape(1, d))


def _matmul_kernel(x_ref, w_ref, o_ref):
    o_ref[...] = jnp.dot(x_ref[...], w_ref[...].astype(BF16),
                         preferred_element_type=F32).astype(o_ref.dtype)


def matmul_cols(x, w, col0, ncols, out_dtype):
    m, k = x.shape
    tm = next(t for t in (2048, 1024, 512, 256, m) if m % t == 0)
    tn = next(t for t in (512, 256, 128) if col0 % t == 0 and ncols % t == 0)
    off = col0 // tn
    return pl.pallas_call(
        _matmul_kernel,
        grid=(m // tm, ncols // tn),
        in_specs=[pl.BlockSpec((tm, k), lambda i, j: (i, 0)),
                  pl.BlockSpec((k, tn), lambda i, j: (0, j + off))],
        out_specs=pl.BlockSpec((tm, tn), lambda i, j: (i, j)),
        out_shape=jax.ShapeDtypeStruct((m, ncols), out_dtype),
        compiler_params=_cparams(("parallel", "arbitrary"), 48),
        name="in_proj",
    )(x, w)


def _rel_bucket(dist):
    n = jnp.maximum(dist, 0)
    max_exact = N_BUCKETS // 2
    large = max_exact + (jnp.log(jnp.maximum(n, 1).astype(F32) / max_exact)
                         / math.log(MAX_DISTANCE / max_exact) * (N_BUCKETS - max_exact)).astype(I32)
    large = jnp.minimum(large, N_BUCKETS - 1)
    return jnp.where(n < max_exact, n, large)


def _bias_by_distance(rel_bias, n):
    d = jnp.arange(n, dtype=I32)
    table = rel_bias.astype(F32)[_rel_bucket(d)]
    far = rel_bias.astype(F32)[N_BUCKETS - 1]
    return (table - far[None, :]).T


def _toeplitz(vals, t):
    h = vals.shape[0]
    w = jnp.concatenate([vals[:, ::-1], jnp.zeros((h, 1), vals.dtype)], axis=1)
    flat = jnp.tile(w, (1, t))[:, :t * (2 * t - 1)]
    return flat.reshape(h, t, 2 * t - 1)[:, :, t - 1:]


def _lambda(lamv_ref):
    lv = lamv_ref[...]
    s1 = jnp.sum(lv[0:1] * lv[1:2], axis=-1, keepdims=True)
    s2 = jnp.sum(lv[2:3] * lv[3:4], axis=-1, keepdims=True)
    return jnp.exp(s1) - jnp.exp(s2) + LAM_INIT


def _split_halves(q):
    lane = lax.broadcasted_iota(I32, q.shape, 1)
    return jnp.concatenate([jnp.where(lane < ATT_DK, q, 0.0), jnp.where(lane >= ATT_DK, q, 0.0)], axis=0)


def _attn_prompt_kernel(lamv_ref, q_ref, k_ref, v_ref, bd_ref, bp_ref, sw_ref, o_ref, kb_ref, v1_ref, *, t, nq):
    kb_ref[...] = k_ref[...].astype(BF16)
    v1_ref[:, :HEAD_W] = v_ref[...].astype(BF16)
    v1_ref[:, HEAD_W:] = jnp.ones((v1_ref.shape[0], HEAD_W), BF16)
    lam = _lambda(lamv_ref)
    sw = sw_ref[...] * (1.0 - LAM_INIT)
    bd = bd_ref[0]
    bp = bp_ref[0]
    bd2 = jnp.concatenate([bd, bd], axis=0)
    bp2 = jnp.concatenate([bp, bp], axis=0)

    for i in range(nq):
        n_keys = (i + 1) * t
        q = q_ref[i * t:(i + 1) * t, :].astype(F32) * ATT_SCALE
        q2 = _split_halves(q).astype(BF16)
        s = lax.dot_general(q2, kb_ref[:n_keys, :], (((1,), (1,)), ((), ())), preferred_element_type=F32)
        near = [s[:, i * t:] + bd2]
        if i >= 1:
            near.insert(0, s[:, (i - 1) * t:i * t] + bp2)
        s = jnp.concatenate(([s[:, :(i - 1) * t]] if i >= 2 else []) + near, axis=1)
        e = jnp.exp(s - jnp.max(s, axis=-1, keepdims=True)).astype(BF16)
        a = jnp.dot(e, v1_ref[:n_keys, :], preferred_element_type=F32)
        a = a[:, :HEAD_W] / a[:, HEAD_W:]
        o = a[:t] - lam * a[t:]
        ms = jnp.mean(o * o, axis=-1, keepdims=True)
        o_ref[i * t:(i + 1) * t, :] = o * lax.rsqrt(ms + SUBLN_EPS) * sw


def attn_prompt(q, k, v, lamv, rel_bias, subln_w, batch, seq):
    t = min(seq, 256)
    nq = seq // t
    assert t >= LANES, "blocks two or more away must lie beyond the last distinct distance bucket"
    bd0 = _bias_by_distance(rel_bias, 2 * t)
    bdiag = _toeplitz(jnp.concatenate([jnp.full((ATT_HEADS, t - 1), NEG, F32), bd0[:, :t]], axis=1), t)
    bprev = _toeplitz(bd0[:, 1:], t)
    blk = lambda: pl.BlockSpec((seq, HEAD_W), lambda b, h: (b, h))
    return pl.pallas_call(
        functools.partial(_attn_prompt_kernel, t=t, nq=nq),
        grid=(batch, ATT_HEADS),
        in_specs=[_full((4, ATT_DK)), blk(), blk(), blk(),
                  pl.BlockSpec((1, t, t), lambda b, h: (h, 0, 0)),
                  pl.BlockSpec((1, t, t), lambda b, h: (h, 0, 0)),
                  _full((1, HEAD_W))],
        out_specs=blk(),
        out_shape=jax.ShapeDtypeStruct((batch * seq, ATT_W), F32),
        scratch_shapes=[pltpu.VMEM((seq, HEAD_W), BF16), pltpu.VMEM((seq, 2 * HEAD_W), BF16)],
        compiler_params=_cparams(("parallel", "parallel"), 48),
        name="attn_prompt",
    )(lamv, q, k, v, bdiag, bprev, subln_w.reshape(1, HEAD_W))


def _attn_decode_kernel(pt_ref, lamv_ref, q_ref, kn_ref, vn_ref, bias_ref, bnew_ref, sw_ref, *rest, pages):
    del pt_ref
    k_refs = rest[:pages]
    v_refs = rest[pages:2 * pages]
    o_ref, m_ref, l_ref, acc_ref = rest[2 * pages:]
    c = pl.program_id(1)
    last = pl.num_programs(1) - 1

    @pl.when(c == 0)
    def _():
        m_ref[...] = jnp.full(m_ref.shape, NEG, F32)
        l_ref[...] = jnp.zeros(l_ref.shape, F32)
        acc_ref[...] = jnp.zeros(acc_ref.shape, F32)

    q16 = _split_halves(q_ref[0] * ATT_SCALE)
    row_head = lax.broadcasted_iota(I32, (2 * ATT_HEADS, 1), 0) & (ATT_HEADS - 1)
    qh = [jnp.where(row_head == h, q16, 0.0).astype(BF16) for h in range(ATT_HEADS)]
    pairs = pages // 2
    st = {"s": [None] * pairs, "pv": jnp.zeros((2 * ATT_HEADS, HEAD_W), F32)}

    def head_rows(ref, h):
        return ref[0, pl.ds(h, PAGE, stride=ATT_HEADS), :]

    def two_pages(refs, pp, h):
        return jnp.concatenate([head_rows(refs[2 * pp], h), head_rows(refs[2 * pp + 1], h)], axis=0).astype(BF16)

    def logits(pp):
        sp = None
        for h in range(ATT_HEADS):
            d = lax.dot_general(qh[h], two_pages(k_refs, pp, h), (((1,), (1,)), ((), ())),
                                preferred_element_type=F32)
            sp = d if sp is None else sp + d
        st["s"][pp] = sp

    def softmax_update():
        s = jnp.concatenate(st["s"], axis=1)
        s = s + bias_ref[...] * (c == last).astype(F32)
        m = m_ref[...]
        mn = jnp.maximum(m, jnp.max(s, axis=-1, keepdims=True))
        st["alpha"] = jnp.exp(m - mn)
        e = jnp.exp(s - mn)
        l_ref[...] = st["alpha"] * l_ref[...] + jnp.sum(e, axis=-1, keepdims=True)
        m_ref[...] = mn
        st["e"] = e.astype(BF16)

    def weighted_values(pp):
        ep = st["e"][:, pp * 2 * PAGE:(pp + 1) * 2 * PAGE]
        for h in range(ATT_HEADS):
            r = jnp.dot(ep, two_pages(v_refs, pp, h), preferred_element_type=F32)
            st["pv"] = st["pv"] + jnp.where(row_head == h, r, 0.0)

    def accumulate():
        acc_ref[...] = st["alpha"] * acc_ref[...] + st["pv"]

    for pp in range(pairs):
        logits(pp)
    softmax_update()
    for pp in range(pairs):
        weighted_values(pp)
    accumulate()

    @pl.when(c == last)
    def _():
        lam = _lambda(lamv_ref)
        m0 = m_ref[...]
        k16 = jnp.concatenate([kn_ref[0], kn_ref[0]], axis=0)
        v16 = jnp.concatenate([vn_ref[0], vn_ref[0]], axis=0)
        s_new = jnp.sum(q16 * k16, axis=-1, keepdims=True) + bnew_ref[...]
        m1 = jnp.maximum(m0, s_new)
        a1 = jnp.exp(m0 - m1)
        e_new = jnp.exp(s_new - m1)
        l1 = a1 * l_ref[...] + e_new
        acc1 = a1 * acc_ref[...] + e_new * v16
        o = acc1[:ATT_HEADS] / l1[:ATT_HEADS] - lam * (acc1[ATT_HEADS:] / l1[ATT_HEADS:])
        ms = jnp.mean(o * o, axis=-1, keepdims=True)
        o_ref[0] = o * lax.rsqrt(ms + SUBLN_EPS) * (sw_ref[...] * (1.0 - LAM_INIT))


def attn_decode(q, k_new, v_new, cache_k, cache_v, page_table, lamv, rel_bias, subln_w):
    nb, n_pages = page_table.shape
    pages = next(g for g in (8, 4, 2) if n_pages % g == 0)
    n_chunks = n_pages // pages
    width = pages * PAGE
    bd0 = _bias_by_distance(rel_bias, width + 1)
    near = bd0[:, width - jnp.arange(width, dtype=I32)]
    bias_last = jnp.concatenate([near, near], axis=0)
    bnew = jnp.concatenate([bd0[:, 0:1], bd0[:, 0:1]], axis=0)
    row = lambda: pl.BlockSpec((1, ATT_HEADS, HEAD_W), lambda b, c, pt: (b, 0, 0))
    page = lambda g: pl.BlockSpec((1, PAGE * ATT_HEADS, HEAD_W), lambda b, c, pt: (pt[b, c * pages + g], 0, 0))
    const = lambda shape: pl.BlockSpec(shape, lambda b, c, pt: (0,) * len(shape))
    grid_spec = pltpu.PrefetchScalarGridSpec(
        num_scalar_prefetch=1,
        grid=(nb, n_chunks),
        in_specs=[const((4, ATT_DK)), row(), row(), row(), const((2 * ATT_HEADS, width)),
                  const((2 * ATT_HEADS, 1)), const((1, HEAD_W))]
                 + [page(g) for g in range(pages)] + [page(g) for g in range(pages)],
        out_specs=row(),
        scratch_shapes=[pltpu.VMEM((2 * ATT_HEADS, 1), F32), pltpu.VMEM((2 * ATT_HEADS, 1), F32),
                        pltpu.VMEM((2 * ATT_HEADS, HEAD_W), F32)],
    )
    return pl.pallas_call(
        functools.partial(_attn_decode_kernel, pages=pages),
        grid_spec=grid_spec,
        out_shape=jax.ShapeDtypeStruct((nb, ATT_HEADS, HEAD_W), F32),
        compiler_params=_cparams(("parallel", "arbitrary"), 56),
        name="attn_decode",
    )(page_table, lamv, q, k_new, v_new, bias_last, bnew, subln_w.reshape(1, HEAD_W),
      *([cache_k] * pages), *([cache_v] * pages))


def _stack_lane_blocks(x, width=LANES):
    return jnp.concatenate([x[:, c * width:(c + 1) * width] for c in range(x.shape[1] // width)], axis=0)


def _unstack_lane_blocks(y, n):
    return jnp.concatenate([y[c * n:(c + 1) * n] for c in range(y.shape[0] // n)], axis=1)


def _segsum(x, r2_ref):
    n = x.shape[0]
    xs = _stack_lane_blocks(x)
    hi = xs.astype(BF16)
    lo = (xs - hi.astype(F32)).astype(BF16)
    y = jnp.dot(jnp.concatenate([hi, lo], axis=1), r2_ref[...], preferred_element_type=F32)
    return _unstack_lane_blocks(y, n)


def _rwkv_prep_kernel(p_ref, sp_ref, mu_ref, w0_ref, w2_ref, a0_ref, a2_ref, g2_ref, kk_ref, ka_ref, rk_ref, r2_ref,
                      a_out, c_out, w_out, b_out, k_out, v_out, vkr_out, bon_out, g_out, prev_ref, *, tt):
    ti = pl.program_id(1)

    @pl.when(ti == 0)
    def _():
        prev_ref[...] = sp_ref[0]

    p = p_ref[0]
    prev_row = prev_ref[...]
    if tt > 1:
        rolled = pltpu.roll(p, 1, axis=0)
        row = lax.broadcasted_iota(I32, p.shape, 0)
        p_prev = jnp.where(row == 0, prev_row, rolled)
    else:
        p_prev = prev_row
    prev_ref[...] = p[tt - 1:tt, :]
    pm = p + mu_ref[...] * (p_prev - p)
    r = pm[:, 0:RW_W]
    kx = pm[:, RW_W:2 * RW_W]
    vx = pm[:, 2 * RW_W:3 * RW_W]
    lora = pm[:, 3 * RW_W:3 * RW_W + RW_LORA_W]
    gl = pm[:, 3 * RW_W + RW_LORA_W:RW_PROJ]
    dw = jnp.dot(jnp.tanh(lora).astype(BF16), w2_ref[...], preferred_element_type=F32)
    w = -jax.nn.softplus(-(w0_ref[...] + dw)) - 0.5
    decay = jnp.exp(-jnp.exp(w))
    a = jax.nn.sigmoid(a0_ref[...] + jnp.dot(lora.astype(BF16), a2_ref[...], preferred_element_type=F32))
    g = jnp.dot(jax.nn.sigmoid(gl).astype(BF16), g2_ref[...], preferred_element_type=F32)
    kk = kx * kk_ref[...]
    kkn = kk / jnp.maximum(jnp.sqrt(_segsum(kk * kk, r2_ref)), 1e-12)
    k = kx * (1.0 + (a - 1.0) * ka_ref[...])
    av = -kkn
    bv = kkn * a
    br = _segsum(bv * r, r2_ref)
    kr = _segsum(k * r, r2_ref)
    a_out[0] = av
    c_out[0] = decay * r + av * br
    w_out[0] = decay
    b_out[0] = bv
    k_out[0] = k
    v_out[0] = vx
    vkr_out[0] = vx * kr
    bon_out[0] = _segsum(r * k * rk_ref[...], r2_ref) * vx
    g_out[0] = g


def _value_diag():
    row = lax.broadcasted_iota(I32, (RW_N, RW_W), 0)
    lane = lax.broadcasted_iota(I32, (RW_N, RW_W), 1)
    return (lane & (RW_N - 1)) == row


def _scan_step(seq_refs, s_ref, t, e_j, zacc, rmat, diag, *, nb, v_terms):
    a_ref, c_ref, w_ref, b_ref, k_ref, v_ref = seq_refs
    width = rmat.shape[0]
    per = (RW_W // width) * RW_N
    row = lambda ref, bb: ref[bb, pl.ds(t, 1), :]
    stacked, pz = [], []
    for bb in range(nb):
        s = s_ref[bb]
        pa = (s * row(a_ref, bb)).astype(BF16)
        pz.append((s * row(c_ref, bb)).astype(BF16))
        v = row(v_ref, bb)
        v_hi = jnp.where(diag, v, 0.0).astype(BF16)
        terms = [pa, v_hi]
        if v_terms == 2:
            terms.append(jnp.where(diag, v - v_hi.astype(F32), 0.0).astype(BF16))
        stacked += [_stack_lane_blocks(x, width) for x in terms]
    red = jnp.dot(jnp.concatenate(stacked, axis=0), rmat, preferred_element_type=F32)
    nterm = 1 + v_terms
    for bb in range(nb):
        parts = [_unstack_lane_blocks(red[(nterm * bb + i) * per:(nterm * bb + i + 1) * per], RW_N)
                 for i in range(nterm)]
        vb = parts[1] if v_terms == 1 else parts[1] + parts[2]
        s_ref[bb] = s_ref[bb] * row(w_ref, bb) + parts[0] * row(b_ref, bb) + vb * row(k_ref, bb)
    return zacc + jnp.dot(jnp.concatenate(pz, axis=0), e_j, preferred_element_type=F32)


def _rwkv_scan_kernel(*refs, nb, tt, nst, v_terms):
    seq_refs = refs[:6]
    s0_ref, r_ref, e_ref, y_ref, so_ref, s_ref = refs[6:]
    ti = pl.program_id(1)

    @pl.when(ti == 0)
    def _():
        s_ref[...] = s0_ref[...]

    diag = _value_diag()
    rmat = r_ref[...]

    def group(g, carry):
        zacc = jnp.zeros((nb * RW_N, LANES), F32)
        for j in range(nst):
            zacc = _scan_step(seq_refs, s_ref, g * nst + j, e_ref[j], zacc, rmat, diag, nb=nb, v_terms=v_terms)
        for bb in range(nb):
            y_ref[bb, g] = zacc[bb * RW_N:(bb + 1) * RW_N]
        return carry

    lax.fori_loop(0, tt // nst, group, 0)

    @pl.when(ti == pl.num_programs(1) - 1)
    def _():
        so_ref[...] = s_ref[...]


def _rwkv_post_kernel(y_ref, vkr_ref, bon_ref, g_ref, lw_ref, lb_ref, r2_ref, o_ref):
    y = y_ref[...] + vkr_ref[...]
    mu = _segsum(y, r2_ref) * (1.0 / RW_N)
    d = y - mu
    var = _segsum(d * d, r2_ref) * (1.0 / RW_N)
    yn = d * lax.rsqrt(var + RW_LN_EPS) * lw_ref[...] + lb_ref[...]
    o_ref[...] = ((yn + bon_ref[...]) * g_ref[...]).astype(BF16)


def _segment_ones(width):
    return jnp.kron(jnp.eye(width // RW_N, dtype=F32), jnp.ones((RW_N, RW_N), F32)).astype(BF16)


def _segsum_operand():
    seg = _segment_ones(LANES)
    return jnp.concatenate([seg, seg], axis=0)


def rwkv_prep(prw, shift_prev, prm, batch, seq):
    tt = min(seq, 128)
    r2 = _segsum_operand()
    zeros64 = jnp.zeros((RW_N, RW_W), F32)
    w2p = jnp.concatenate([prm["rw_w2"], zeros64], axis=0).astype(BF16)
    a2p = jnp.concatenate([zeros64, prm["rw_a2"]], axis=0).astype(BF16)
    row = lambda a: a.reshape(1, -1)
    blk_in = pl.BlockSpec((1, tt, RW_PROJ), lambda b, i: (b, i, 0))
    blk_out = pl.BlockSpec((1, tt, RW_W), lambda b, i: (b, i, 0))
    c2 = lambda shape: pl.BlockSpec(shape, lambda b, i: (0,) * len(shape))
    return pl.pallas_call(
        functools.partial(_rwkv_prep_kernel, tt=tt),
        grid=(batch, seq // tt),
        in_specs=[blk_in, pl.BlockSpec((1, 1, RW_PROJ), lambda b, i: (b, 0, 0)), c2((1, RW_PROJ)),
                  c2((1, RW_W)), c2((RW_LORA_W, RW_W)), c2((1, RW_W)), c2((RW_LORA_W, RW_W)),
                  c2((RW_GATE_W, RW_W)), c2((1, RW_W)), c2((1, RW_W)), c2((1, RW_W)), c2((2 * LANES, LANES))],
        out_specs=[blk_out] * 9,
        out_shape=[jax.ShapeDtypeStruct((batch, seq, RW_W), F32)] * 9,
        scratch_shapes=[pltpu.VMEM((1, RW_PROJ), F32)],
        compiler_params=_cparams(("parallel", "arbitrary"), 56),
        name="rwkv_prep",
    )(prw.reshape(batch, seq, RW_PROJ), shift_prev.reshape(batch, 1, RW_PROJ), row(prm["rw_mu"]),
      row(prm["rw_w0"]), w2p, row(prm["rw_a0"]), a2p, prm["rw_g2"].astype(BF16), row(prm["rw_k_k"]),
      row(prm["rw_k_a"]), row(prm["rw_r_k"]), r2)


SCAN_NB = 4


def _scan_group(seq):
    return min(seq, LANES // RW_HEADS)


def _scan_operands(wkv0, batch, seq):
    nst = _scan_group(seq)
    head = jnp.arange(RW_W, dtype=I32) // RW_N
    ecomp = (head[None, :, None] + RW_HEADS * jnp.arange(nst, dtype=I32)[:, None, None]
             == jnp.arange(LANES, dtype=I32)[None, None, :]).astype(BF16)
    s0 = wkv0.astype(F32).transpose(0, 2, 1, 3).reshape(batch, RW_N, RW_W)
    return s0, _segment_ones(MXU_DIM), ecomp


def _scan_out_shapes(batch, seq):
    return [jax.ShapeDtypeStruct((batch, seq // _scan_group(seq), RW_N, LANES), F32),
            jax.ShapeDtypeStruct((batch, RW_N, RW_W), F32)]


def rwkv_scan(seq_arrays, wkv0, batch, seq):
    nst = _scan_group(seq)
    ts = min(seq, 64)
    nb = SCAN_NB
    v_terms = 2 if seq == 1 else 1
    c2 = lambda shape: pl.BlockSpec(shape, lambda b, i: (0,) * len(shape))
    seq_blk = pl.BlockSpec((nb, ts, RW_W), lambda b, i: (b, i, 0))
    st_blk = pl.BlockSpec((nb, RW_N, RW_W), lambda b, i: (b, 0, 0))
    return pl.pallas_call(
        functools.partial(_rwkv_scan_kernel, nb=nb, tt=ts, nst=nst, v_terms=v_terms),
        grid=(batch // nb, seq // ts),
        in_specs=[seq_blk] * 6 + [st_blk, c2((MXU_DIM, MXU_DIM)), c2((nst, RW_W, LANES))],
        out_specs=[pl.BlockSpec((nb, ts // nst, RW_N, LANES), lambda b, i: (b, i, 0, 0)), st_blk],
        out_shape=_scan_out_shapes(batch, seq),
        scratch_shapes=[pltpu.VMEM((nb, RW_N, RW_W), F32)],
        compiler_params=_cparams(("parallel", "arbitrary"), 48),
        name="rwkv_scan",
    )(*seq_arrays, *_scan_operands(wkv0, batch, seq))


def rwkv_finish(yc, s_fin, vkr, bon, gv, prm, batch, seq):
    m = batch * seq
    nst = _scan_group(seq)
    row = lambda a: a.reshape(1, -1)
    yz = yc.reshape(batch, seq // nst, RW_N, LANES // RW_HEADS, RW_HEADS)[:, :, :, :nst, :]
    yz = yz.transpose(0, 1, 3, 4, 2).reshape(m, RW_W)
    wkv_new = s_fin.reshape(batch, RW_N, RW_HEADS, RW_N).transpose(0, 2, 1, 3)

    tm = min(m, ROW_TILE)
    rows = pl.BlockSpec((tm, RW_W), lambda i: (i, 0))
    y = pl.pallas_call(
        _rwkv_post_kernel,
        grid=(m // tm,),
        in_specs=[rows] * 4 + [_full((1, RW_W)), _full((1, RW_W)), _full((2 * LANES, LANES))],
        out_specs=rows,
        out_shape=jax.ShapeDtypeStruct((m, RW_W), BF16),
        compiler_params=_cparams(("parallel",), 48),
        name="rwkv_post",
    )(yz, vkr.reshape(m, RW_W), bon.reshape(m, RW_W), gv.reshape(m, RW_W), row(prm["rw_ln_w"]),
      row(prm["rw_ln_b"]), _segsum_operand())
    return y, wkv_new


def rwkv_mix(prw, shift_prev, wkv0, prm, batch, seq):
    *seq_arrays, vkr, bon, gv = rwkv_prep(prw, shift_prev, prm, batch, seq)
    yc, s_fin = rwkv_scan(seq_arrays[:6], wkv0, batch, seq)
    return rwkv_finish(yc, s_fin, vkr, bon, gv, prm, batch, seq)


def _merge_kernel(ya_ref, yb_ref, ga_ref, gb_ref, ba_ref, bb_ref, wa_ref, wb_ref, o_ref):
    pa = jnp.dot(ya_ref[...].astype(BF16), wa_ref[...], preferred_element_type=F32)
    pb = jnp.dot(yb_ref[...], wb_ref[...], preferred_element_type=F32)
    ga = jax.nn.sigmoid(ga_ref[...] + ba_ref[...])
    gb = jax.nn.sigmoid(gb_ref[...] + bb_ref[...])
    o_ref[...] = (ga * pa + gb * pb).astype(BF16)


def merge_branches(y_att, y_rw, g_lin, b_gate, wa, wb):
    m = y_att.shape[0]
    tm = min(m, ROW_TILE)
    d = D_MODEL
    return pl.pallas_call(
        _merge_kernel,
        grid=(m // tm,),
        in_specs=[pl.BlockSpec((tm, ATT_W), lambda i: (i, 0)), pl.BlockSpec((tm, RW_W), lambda i: (i, 0)),
                  pl.BlockSpec((tm, d), lambda i: (i, 0)), pl.BlockSpec((tm, d), lambda i: (i, 1)),
                  pl.BlockSpec((1, d), lambda i: (0, 0)), pl.BlockSpec((1, d), lambda i: (0, 1)),
                  _full((ATT_W, d)), _full((RW_W, d))],
        out_specs=pl.BlockSpec((tm, d), lambda i: (i, 0)),
        out_shape=jax.ShapeDtypeStruct((m, d), BF16),
        compiler_params=_cparams(("parallel",), 48),
        name="merge_branches",
    )(y_att, y_rw, g_lin, g_lin, b_gate.reshape(1, 2 * d), b_gate.reshape(1, 2 * d), wa, wb)


SLAB = D_MODEL // LANES


def _store_token_slabs(ref, x):
    n = x.shape[0]
    for c in range(SLAB):
        ref[pl.ds(c, n, stride=SLAB), :] = x[:, c * LANES:(c + 1) * LANES]


def _load_token_slabs(ref, n):
    return jnp.concatenate([ref[pl.ds(c, n, stride=SLAB), :] for c in range(SLAB)], axis=1)


def _outproj_router_kernel(mg_ref, wo_ref, x_ref, nw_ref, wr_ref, br_ref, h_ref, hn_ref, eid_ref, ew_ref):
    h = x_ref[...] + jnp.dot(mg_ref[...], wo_ref[...], preferred_element_type=F32)
    h_ref[...] = h
    ms = jnp.mean(h * h, axis=-1, keepdims=True)
    hn = h * lax.rsqrt(ms + NORM_EPS) * nw_ref[...]
    _store_token_slabs(hn_ref, hn)
    lg = jnp.dot(hn.astype(BF16), wr_ref[...], preferred_element_type=F32) + br_ref[...]
    lane = lax.broadcasted_iota(I32, lg.shape, 1)
    lane_f = lane.astype(F32)
    big = float(LANES)
    lgm = jnp.where(lane < N_GROUPS, lg, NEG)
    gmax = jnp.max(lgm, axis=-1, keepdims=True)
    gidx = jnp.min(jnp.where(lgm == gmax, lane_f, big), axis=-1, keepdims=True)
    g_w = 1.0 / jnp.sum(jnp.exp(lgm - gmax), axis=-1, keepdims=True)
    lane_group = ((lane >> 3) - 1).astype(F32)
    in_group = (lane >= N_GROUPS) & (lane < N_GROUPS + N_EXPERTS) & (lane_group == gidx)
    le = jnp.where(in_group, lg, NEG)
    m1 = jnp.max(le, axis=-1, keepdims=True)
    i1 = jnp.min(jnp.where(le == m1, lane_f, big), axis=-1, keepdims=True)
    le2 = jnp.where(lane_f == i1, NEG, le)
    m2 = jnp.max(le2, axis=-1, keepdims=True)
    i2 = jnp.min(jnp.where(le2 == m2, lane_f, big), axis=-1, keepdims=True)
    e2 = jnp.exp(m2 - m1)
    w1 = g_w / (1.0 + e2)
    w2 = g_w * e2 / (1.0 + e2)
    col = lax.broadcasted_iota(I32, eid_ref.shape, 1)
    eid_ref[...] = jnp.where(col == 0, i1, i2).astype(I32) - N_GROUPS
    ew_ref[...] = jnp.where(col == 0, w1, w2)


def outproj_router(merged, w_out, x, norm_w, w_router, b_router):
    m, d = x.shape
    tm = min(m, ROW_TILE)
    rows = lambda w: pl.BlockSpec((tm, w), lambda i: (i, 0))
    return pl.pallas_call(
        _outproj_router_kernel,
        grid=(m // tm,),
        in_specs=[rows(d), _full((d, d)), rows(d), _full((1, d)), _full((d, LANES)), _full((1, LANES))],
        out_specs=[rows(d), pl.BlockSpec((tm * SLAB, LANES), lambda i: (i, 0)), rows(2), rows(2)],
        out_shape=[jax.ShapeDtypeStruct((m, d), F32), jax.ShapeDtypeStruct((m * SLAB, LANES), F32),
                   jax.ShapeDtypeStruct((m, 2), I32), jax.ShapeDtypeStruct((m, 2), F32)],
        compiler_params=_cparams(("parallel",), 48),
        name="outproj_router",
    )(merged, w_out, x, norm_w.reshape(1, d), w_router, b_router)


def _moe_kernel(starts_ref, counts_ref, pcounts_ref, src_ref, hp_ref, hs_ref, wg_ref, wu_ref, wd_ref,
                yp_ref, ys_ref, wgb, wub, wdb, xbuf, ybuf, sem_in, sem_out, *, n_prompt, n_sample):
    e = pl.program_id(0)

    @pl.when(e == 0)
    def _():
        xbuf[...] = jnp.zeros(xbuf.shape, F32)

    n = counts_ref[e]
    n_p = pcounts_ref[e]
    start = starts_ref[e]
    nblk = (n + MOE_BLK - 1) // MOE_BLK

    def slab(i):
        return pl.ds(pl.multiple_of(i * SLAB, SLAB), SLAB)

    def row_in(src_hbm, tok, slot, r):
        return pltpu.make_async_copy(src_hbm.at[slab(tok)], xbuf.at[slot, slab(r)], sem_in.at[slot])

    def row_out(dst_hbm, dst, r):
        return pltpu.make_async_copy(ybuf.at[slab(r)], dst_hbm.at[slab(dst)], sem_out)

    def for_rows(lo, hi, body):
        whole = (lo == 0) & (hi == MOE_BLK)

        @pl.when(whole)
        def _():
            def chunk(i, c):
                for u in range(MOE_UNROLL):
                    body(i * MOE_UNROLL + u)
                return c
            lax.fori_loop(0, MOE_BLK // MOE_UNROLL, chunk, 0)

        @pl.when(jnp.logical_not(whole))
        def _():
            lax.fori_loop(lo, hi, lambda r, c: (body(r), c)[1], 0)

    def block_rows(bi):
        nv = jnp.minimum(MOE_BLK, n - bi * MOE_BLK)
        nvp = jnp.clip(n_p - bi * MOE_BLK, 0, nv)
        return start + bi * MOE_BLK, nv, nvp

    def gather_start(bi, slot):
        r0, nv, nvp = block_rows(bi)
        for_rows(0, nvp, lambda r: row_in(hp_ref, src_ref[r0 + r] >> 1, slot, r).start())
        for_rows(nvp, nv, lambda r: row_in(hs_ref, (src_ref[r0 + r] >> 1) - n_prompt, slot, r).start())

    def gather_wait(bi, slot):
        _, nv, _ = block_rows(bi)
        for_rows(0, nv, lambda r: row_in(hp_ref, 0, slot, r).wait())

    def scatter_start(bi):
        r0, nv, nvp = block_rows(bi)

        def to_prompt(r):
            src = src_ref[r0 + r]
            row_out(yp_ref, (src & 1) * n_prompt + (src >> 1), r).start(priority=WRITEBACK_DMA_PRIORITY)

        def to_sample(r):
            src = src_ref[r0 + r]
            row_out(ys_ref, (src & 1) * n_sample + (src >> 1) - n_prompt, r).start(priority=WRITEBACK_DMA_PRIORITY)

        for_rows(0, nvp, to_prompt)
        for_rows(nvp, nv, to_sample)

    def scatter_wait(bi):
        _, nv, _ = block_rows(bi)
        for_rows(0, nv, lambda r: row_out(yp_ref, 0, r).wait())

    @pl.when(n > 0)
    def _():
        gather_start(0, 0)
        wgb[...] = wg_ref[0].astype(BF16)
        wub[...] = wu_ref[0].astype(BF16)
        wdb[...] = wd_ref[0].astype(BF16)

        def block(bi, carry):
            slot = bi & 1
            gather_wait(bi, slot)

            @pl.when(bi + 1 < nblk)
            def _():
                gather_start(bi + 1, 1 - slot)

            x = _load_token_slabs(xbuf.at[slot], MOE_BLK).astype(BF16)
            hg = jnp.dot(x, wgb[...], preferred_element_type=F32)
            hu = jnp.dot(x, wub[...], preferred_element_type=F32)
            hd = (hg * jax.nn.sigmoid(hg) * hu).astype(BF16)
            y = jnp.dot(hd, wdb[...], preferred_element_type=F32)

            @pl.when(bi > 0)
            def _():
                scatter_wait(bi - 1)

            _store_token_slabs(ybuf, y)
            scatter_start(bi)
            return carry

        lax.fori_loop(0, nblk, block, 0)
        scatter_wait(nblk - 1)


def moe_experts(hn_p, hn_s, eid, w_gate, w_up, w_down):
    d = D_MODEL
    n_prompt = hn_p.shape[0] // SLAB
    n_sample = hn_s.shape[0] // SLAB
    n_rows = eid.shape[0] * 2
    flat_e = eid.reshape(n_rows)
    onehot = (flat_e[:, None] == jnp.arange(N_EXPERTS, dtype=I32)[None, :]).astype(I32)
    counts = jnp.sum(onehot, axis=0)
    pcounts = jnp.sum(onehot[:2 * n_prompt], axis=0)
    starts = jnp.cumsum(counts) - counts
    rank = jnp.sum(onehot * (jnp.cumsum(onehot, axis=0) - 1), axis=1)
    pos = starts[flat_e] + rank
    src = jnp.zeros((n_rows,), I32).at[pos].set(jnp.arange(n_rows, dtype=I32), unique_indices=True,
                                                mode="promise_in_bounds")
    any_spec = pl.BlockSpec(memory_space=pl.ANY)
    grid_spec = pltpu.PrefetchScalarGridSpec(
        num_scalar_prefetch=4,
        grid=(N_EXPERTS,),
        in_specs=[any_spec, any_spec,
                  pl.BlockSpec((1, d, D_FF), lambda e, *_: (e, 0, 0)),
                  pl.BlockSpec((1, d, D_FF), lambda e, *_: (e, 0, 0)),
                  pl.BlockSpec((1, D_FF, d), lambda e, *_: (e, 0, 0))],
        out_specs=[any_spec, any_spec],
        scratch_shapes=[pltpu.VMEM((d, D_FF), BF16), pltpu.VMEM((d, D_FF), BF16), pltpu.VMEM((D_FF, d), BF16),
                        pltpu.VMEM((2, MOE_BLK * SLAB, LANES), F32), pltpu.VMEM((MOE_BLK * SLAB, LANES), F32),
                        pltpu.SemaphoreType.DMA((2,)), pltpu.SemaphoreType.DMA(())],
    )
    return pl.pallas_call(
        functools.partial(_moe_kernel, n_prompt=n_prompt, n_sample=n_sample),
        grid_spec=grid_spec,
        out_shape=[jax.ShapeDtypeStruct((2 * n_prompt * SLAB, LANES), F32),
                   jax.ShapeDtypeStruct((2 * n_sample * SLAB, LANES), F32)],
        compiler_params=_cparams(("arbitrary",), 56),
        name="moe_experts",
    )(starts, counts, pcounts, src, hn_p, hn_s, w_gate, w_up, w_down)


def _final_kernel(h_ref, y0_ref, y1_ref, ew_ref, nw_ref, o_ref):
    ew = ew_ref[...]
    tm = h_ref.shape[0]
    x = (h_ref[...] + ew[:, 0:1] * _load_token_slabs(y0_ref, tm) + ew[:, 1:2] * _load_token_slabs(y1_ref, tm))
    ms = jnp.mean(x * x, axis=-1, keepdims=True)
    o_ref[...] = x * lax.rsqrt(ms + NORM_EPS) * nw_ref[...]


def combine_final(h, y_rows, ew, norm_w):
    m, d = h.shape
    tm = min(m, ROW_TILE)
    o1 = m // tm
    return pl.pallas_call(
        _final_kernel,
        grid=(m // tm,),
        in_specs=[pl.BlockSpec((tm, d), lambda i: (i, 0)),
                  pl.BlockSpec((tm * SLAB, LANES), lambda i: (i, 0)),
                  pl.BlockSpec((tm * SLAB, LANES), lambda i: (i + o1, 0)),
                  pl.BlockSpec((tm, 2), lambda i: (i, 0)), _full((1, d))],
        out_specs=pl.BlockSpec((tm, d), lambda i: (i, 0)),
        out_shape=jax.ShapeDtypeStruct((m, d), F32),
        compiler_params=_cparams(("parallel",), 48),
        name="combine_final",
    )(h, y_rows, y_rows, ew, norm_w.reshape(1, d))


_RW_COL0 = 3 * ATT_W
_GATE_COL0 = _RW_COL0 + RW_PROJ


def _in_proj(x2, prm, q_dtype):
    xn = rmsnorm_bf16(x2, prm["norm_mix_w"])
    w_in = prm["w_in"]
    q = matmul_cols(xn, w_in, 0, ATT_W, q_dtype)
    k = matmul_cols(xn, w_in, ATT_W, ATT_W, F32)
    v = matmul_cols(xn, w_in, 2 * ATT_W, ATT_W, F32)
    prw = matmul_cols(xn, w_in, _RW_COL0, RW_PROJ, F32)
    g_lin = matmul_cols(xn, w_in, _GATE_COL0, 2 * D_MODEL, F32)
    return q, k, v, prw, g_lin


def _mixer_tail(x2, y_att, y_rw, g_lin, prm):
    merged = merge_branches(y_att, y_rw, g_lin, prm["b_gate"], prm["wa_bf16"], prm["wb_bf16"])
    return outproj_router(merged, prm["wo_bf16"], x2, prm["norm_ffn_w"], prm["w_router"], prm["b_router"])


def kernel(x_prompt, x_sample, cache_k, cache_v, state_wkv, state_shift, page_table, norm_mix_w, w_in, b_gate, lambda_q1, lambda_k1, lambda_q2, lambda_k2, subln_w, rel_bias, rw_mu, rw_w0, rw_w2, rw_a0, rw_a2, rw_g2, rw_k_k, rw_k_a, rw_r_k, rw_ln_w, rw_ln_b, w_branch_a, w_branch_b, w_out, norm_ffn_w, w_router_g, b_router_g, w_router_e, b_router_e, w_gate_e, w_up_e, w_down_e, norm_final_w):
    assert w_in.shape[0] == 1, "single-layer trunk"
    bp, sp, d = x_prompt.shape
    bs, ss, _ = x_sample.shape
    assert ss == 1 and d == D_MODEL
    n_pool = cache_k.shape[1]
    pad_lanes = LANES - N_GROUPS - N_EXPERTS
    prm = {
        "norm_mix_w": norm_mix_w[0], "w_in": w_in[0], "b_gate": b_gate[0],
        "rw_mu": rw_mu[0], "rw_w0": rw_w0[0], "rw_w2": rw_w2[0], "rw_a0": rw_a0[0], "rw_a2": rw_a2[0],
        "rw_g2": rw_g2[0], "rw_k_k": rw_k_k[0], "rw_k_a": rw_k_a[0], "rw_r_k": rw_r_k[0].reshape(RW_W),
        "rw_ln_w": rw_ln_w[0], "rw_ln_b": rw_ln_b[0],
        "wa_bf16": w_branch_a[0].astype(BF16), "wb_bf16": w_branch_b[0].astype(BF16),
        "wo_bf16": w_out[0].astype(BF16), "norm_ffn_w": norm_ffn_w[0],
        "w_router": jnp.concatenate([w_router_g[0], w_router_e[0], jnp.zeros((d, pad_lanes), F32)],
                                    axis=1).astype(BF16),
        "b_router": jnp.concatenate([b_router_g[0], b_router_e[0], jnp.zeros((pad_lanes,), F32)]).reshape(1, LANES),
    }
    lamv = jnp.stack([lambda_q1[0], lambda_k1[0], lambda_q2[0], lambda_k2[0]]).astype(F32)
    sub_w = subln_w[0]

    mp = bp * sp
    xp2 = x_prompt.reshape(mp, d)
    q_p, k_p, v_p, prw_p, g_p = _in_proj(xp2, prm, BF16)
    att_p = attn_prompt(q_p, k_p, v_p, lamv, rel_bias, sub_w, bp, sp)
    rw_p, wkv_p = rwkv_mix(prw_p, jnp.zeros((bp, RW_PROJ), F32), jnp.zeros((bp, RW_HEADS, RW_N, RW_N), F32),
                           prm, bp, sp)
    h_p, hn_p, eid_p, ew_p = _mixer_tail(xp2, att_p, rw_p, g_p, prm)

    xs2 = x_sample.reshape(bs, d)
    q_s, k_s, v_s, prw_s, g_s = _in_proj(xs2, prm, F32)
    heads = lambda a: a.reshape(bs, ATT_HEADS, HEAD_W)
    att_s = attn_decode(heads(q_s), heads(k_s), heads(v_s),
                        cache_k[0].reshape(n_pool, PAGE * ATT_HEADS, HEAD_W),
                        cache_v[0].reshape(n_pool, PAGE * ATT_HEADS, HEAD_W),
                        page_table, lamv, rel_bias, sub_w).reshape(bs, ATT_W)
    rw_s, wkv_s = rwkv_mix(prw_s, state_shift[0], state_wkv[0], prm, bs, 1)
    h_s, hn_s, eid_s, ew_s = _mixer_tail(xs2, att_s, rw_s, g_s, prm)

    yr_p, yr_s = moe_experts(hn_p, hn_s, jnp.concatenate([eid_p, eid_s], axis=0),
                             w_gate_e[0], w_up_e[0], w_down_e[0])
    y_p = combine_final(h_p, yr_p, ew_p, norm_final_w)
    y_s = combine_final(h_s, yr_s, ew_s, norm_final_w)

    return (y_p.reshape(bp, sp, d), y_s.reshape(bs, 1, d),
            k_p.reshape(1, bp, sp, ATT_HEADS, HEAD_W), v_p.reshape(1, bp, sp, ATT_HEADS, HEAD_W),
            wkv_p.astype(state_wkv.dtype)[None], prw_p.reshape(bp, sp, RW_PROJ)[None, :, -1],
            k_s.reshape(1, bs, 1, ATT_HEADS, HEAD_W), v_s.reshape(1, bs, 1, ATT_HEADS, HEAD_W),
            wkv_s.astype(state_wkv.dtype)[None], prw_s.reshape(1, bs, RW_PROJ))
```

```python
import functools
import math

import jax
import jax.numpy as jnp
from jax import lax
from jax.experimental import pallas as pl
from jax.experimental.pallas import tpu as pltpu

F32 = jnp.float32
BF16 = jnp.bfloat16
I32 = jnp.int32

D_MODEL = 2048
ATT_HEADS = 8
ATT_DK = 64
HEAD_W = 2 * ATT_DK
ATT_W = ATT_HEADS * HEAD_W
ATT_SCALE = ATT_DK ** -0.5
RW_HEADS = 16
RW_N = 64
RW_W = RW_HEADS * RW_N
RW_LORA_W = 128
RW_GATE_W = 128
RW_PROJ = 3 * RW_W + RW_LORA_W + RW_GATE_W
N_BUCKETS = 32
MAX_DISTANCE = 128
PAGE = 128
EXPERTS_PER_GROUP = 8
N_GROUPS = 8
N_EXPERTS = N_GROUPS * EXPERTS_PER_GROUP
D_FF = 512
NORM_EPS = 1e-6
SUBLN_EPS = 1e-5
RW_LN_EPS = 64e-5
LAM_INIT = 0.8 - 0.6 * math.exp(-0.3 * 0)
NEG = -1e30
LANES = 128
MXU_DIM = 256
MOE_BLK = 128
MOE_UNROLL = 16
ROW_DMA_PRIORITY = 1
ROW_TILE = 256
MIB = 1024 * 1024


def _cparams(sem, vmem_mib):
    return pltpu.CompilerParams(dimension_semantics=sem, vmem_limit_bytes=vmem_mib * MIB)


def _full(shape):
    nd = len(shape)
    return pl.BlockSpec(shape, lambda *_: (0,) * nd)


def _rmsnorm_bf16_kernel(x_ref, w_ref, o_ref):
    x = x_ref[...]
    ms = jnp.mean(x * x, axis=-1, keepdims=True)
    o_ref[...] = (x * lax.rsqrt(ms + NORM_EPS) * w_ref[...]).astype(BF16)


def rmsnorm_bf16(x, w):
    m, d = x.shape
    tm = min(m, 512)
    return pl.pallas_call(
        _rmsnorm_bf16_kernel,
        grid=(m // tm,),
        in_specs=[pl.BlockSpec((tm, d), lambda i: (i, 0)), _full((1, d))],
        out_specs=pl.BlockSpec((tm, d), lambda i: (i, 0)),
        out_shape=jax.ShapeDtypeStruct((m, d), BF16),
        compiler_params=_cparams(("parallel",), 32),
        name="rmsnorm_bf16",
    )(x, w.reshape(1, d))


def _matmul_kernel(x_ref, w_ref, o_ref):
    o_ref[...] = jnp.dot(x_ref[...], w_ref[...].astype(BF16),
                         preferred_element_type=F32).astype(o_ref.dtype)


def matmul_cols(x, w, col0, ncols, out_dtype):
    m, k = x.shape
    tm = next(t for t in (2048, 1024, 512, 256, m) if m % t == 0)
    tn = next(t for t in (512, 256, 128) if col0 % t == 0 and ncols % t == 0)
    off = col0 // tn
    return pl.pallas_call(
        _matmul_kernel,
        grid=(m // tm, ncols // tn),
        in_specs=[pl.BlockSpec((tm, k), lambda i, j: (i, 0)),
                  pl.BlockSpec((k, tn), lambda i, j: (0, j + off))],
        out_specs=pl.BlockSpec((tm, tn), lambda i, j: (i, j)),
        out_shape=jax.ShapeDtypeStruct((m, ncols), out_dtype),
        compiler_params=_cparams(("parallel", "arbitrary"), 48),
        name="in_proj",
    )(x, w)


def _rel_bucket(dist):
    n = jnp.maximum(dist, 0)
    max_exact = N_BUCKETS // 2
    large = max_exact + (jnp.log(jnp.maximum(n, 1).astype(F32) / max_exact)
                         / math.log(MAX_DISTANCE / max_exact) * (N_BUCKETS - max_exact)).astype(I32)
    large = jnp.minimum(large, N_BUCKETS - 1)
    return jnp.where(n < max_exact, n, large)


def _bias_by_distance(rel_bias, n):
    d = jnp.arange(n, dtype=I32)
    table = rel_bias.astype(F32)[_rel_bucket(d)]
    far = rel_bias.astype(F32)[N_BUCKETS - 1]
    return (table - far[None, :]).T


def _toeplitz(vals, t):
    h = vals.shape[0]
    w = jnp.concatenate([vals[:, ::-1], jnp.zeros((h, 1), vals.dtype)], axis=1)
    flat = jnp.tile(w, (1, t))[:, :t * (2 * t - 1)]
    return flat.reshape(h, t, 2 * t - 1)[:, :, t - 1:]


def _lambda(lamv_ref):
    lv = lamv_ref[...]
    s1 = jnp.sum(lv[0:1] * lv[1:2], axis=-1, keepdims=True)
    s2 = jnp.sum(lv[2:3] * lv[3:4], axis=-1, keepdims=True)
    return jnp.exp(s1) - jnp.exp(s2) + LAM_INIT


def _split_halves(q):
    lane = lax.broadcasted_iota(I32, q.shape, 1)
    return jnp.concatenate([jnp.where(lane < ATT_DK, q, 0.0), jnp.where(lane >= ATT_DK, q, 0.0)], axis=0)


def _attn_prompt_kernel(lamv_ref, q_ref, k_ref, v_ref, bd_ref, bp_ref, sw_ref, o_ref, kb_ref, v1_ref, *, t, nq):
    kb_ref[...] = k_ref[...].astype(BF16)
    v1_ref[:, :HEAD_W] = v_ref[...].astype(BF16)
    v1_ref[:, HEAD_W:] = jnp.ones((v1_ref.shape[0], HEAD_W), BF16)
    lam = _lambda(lamv_ref)
    sw = sw_ref[...] * (1.0 - LAM_INIT)
    bd = bd_ref[0]
    bp = bp_ref[0]
    bd2 = jnp.concatenate([bd, bd], axis=0)
    bp2 = jnp.concatenate([bp, bp], axis=0)

    for i in range(nq):
        n_keys = (i + 1) * t
        q = q_ref[i * t:(i + 1) * t, :].astype(F32) * ATT_SCALE
        q2 = _split_halves(q).astype(BF16)
        s = lax.dot_general(q2, kb_ref[:n_keys, :], (((1,), (1,)), ((), ())), preferred_element_type=F32)
        near = [s[:, i * t:] + bd2]
        if i >= 1:
            near.insert(0, s[:, (i - 1) * t:i * t] + bp2)
        s = jnp.concatenate(([s[:, :(i - 1) * t]] if i >= 2 else []) + near, axis=1)
        e = jnp.exp(s - jnp.max(s, axis=-1, keepdims=True)).astype(BF16)
        a = jnp.dot(e, v1_ref[:n_keys, :], preferred_element_type=F32)
        a = a[:, :HEAD_W] / a[:, HEAD_W:]
        o = a[:t] - lam * a[t:]
        ms = jnp.mean(o * o, axis=-1, keepdims=True)
        o_ref[i * t:(i + 1) * t, :] = o * lax.rsqrt(ms + SUBLN_EPS) * sw


def attn_prompt(q, k, v, lamv, rel_bias, subln_w, batch, seq):
    t = min(seq, 256)
    nq = seq // t
    assert t >= LANES, "blocks two or more away must lie beyond the last distinct distance bucket"
    bd0 = _bias_by_distance(rel_bias, 2 * t)
    bdiag = _toeplitz(jnp.concatenate([jnp.full((ATT_HEADS, t - 1), NEG, F32), bd0[:, :t]], axis=1), t)
    bprev = _toeplitz(bd0[:, 1:], t)
    blk = lambda: pl.BlockSpec((seq, HEAD_W), lambda b, h: (b, h))
    return pl.pallas_call(
        functools.partial(_attn_prompt_kernel, t=t, nq=nq),
        grid=(batch, ATT_HEADS),
        in_specs=[_full((4, ATT_DK)), blk(), blk(), blk(),
                  pl.BlockSpec((1, t, t), lambda b, h: (h, 0, 0)),
                  pl.BlockSpec((1, t, t), lambda b, h: (h, 0, 0)),
                  _full((1, HEAD_W))],
        out_specs=blk(),
        out_shape=jax.ShapeDtypeStruct((batch * seq, ATT_W), F32),
        scratch_shapes=[pltpu.VMEM((seq, HEAD_W), BF16), pltpu.VMEM((seq, 2 * HEAD_W), BF16)],
        compiler_params=_cparams(("parallel", "parallel"), 48),
        name="attn_prompt",
    )(lamv, q, k, v, bdiag, bprev, subln_w.reshape(1, HEAD_W))


def _attn_decode_kernel(pt_ref, lamv_ref, q_ref, kn_ref, vn_ref, bias_ref, bnew_ref, sw_ref, *rest, pages):
    del pt_ref
    k_refs = rest[:pages]
    v_refs = rest[pages:2 * pages]
    o_ref, m_ref, l_ref, acc_ref = rest[2 * pages:]
    c = pl.program_id(1)
    last = pl.num_programs(1) - 1

    @pl.when(c == 0)
    def _():
        m_ref[...] = jnp.full(m_ref.shape, NEG, F32)
        l_ref[...] = jnp.zeros(l_ref.shape, F32)
        acc_ref[...] = jnp.zeros(acc_ref.shape, F32)

    q16 = _split_halves(q_ref[0] * ATT_SCALE)
    row_head = lax.broadcasted_iota(I32, (2 * ATT_HEADS, 1), 0) & (ATT_HEADS - 1)
    qh = [jnp.where(row_head == h, q16, 0.0).astype(BF16) for h in range(ATT_HEADS)]
    pairs = pages // 2
    st = {"s": [None] * pairs, "pv": jnp.zeros((2 * ATT_HEADS, HEAD_W), F32)}

    def head_rows(ref, h):
        return ref[0, pl.ds(h, PAGE, stride=ATT_HEADS), :]

    def two_pages(refs, pp, h):
        return jnp.concatenate([head_rows(refs[2 * pp], h), head_rows(refs[2 * pp + 1], h)], axis=0).astype(BF16)

    def logits(pp):
        sp = None
        for h in range(ATT_HEADS):
            d = lax.dot_general(qh[h], two_pages(k_refs, pp, h), (((1,), (1,)), ((), ())),
                                preferred_element_type=F32)
            sp = d if sp is None else sp + d
        st["s"][pp] = sp

    def softmax_update():
        s = jnp.concatenate(st["s"], axis=1)
        s = s + bias_ref[...] * (c == last).astype(F32)
        m = m_ref[...]
        mn = jnp.maximum(m, jnp.max(s, axis=-1, keepdims=True))
        st["alpha"] = jnp.exp(m - mn)
        e = jnp.exp(s - mn)
        l_ref[...] = st["alpha"] * l_ref[...] + jnp.sum(e, axis=-1, keepdims=True)
        m_ref[...] = mn
        st["e"] = e.astype(BF16)

    def weighted_values(pp):
        ep = st["e"][:, pp * 2 * PAGE:(pp + 1) * 2 * PAGE]
        for h in range(ATT_HEADS):
            r = jnp.dot(ep, two_pages(v_refs, pp, h), preferred_element_type=F32)
            st["pv"] = st["pv"] + jnp.where(row_head == h, r, 0.0)

    def accumulate():
        acc_ref[...] = st["alpha"] * acc_ref[...] + st["pv"]

    for pp in range(pairs):
        logits(pp)
    softmax_update()
    for pp in range(pairs):
        weighted_values(pp)
    accumulate()

    @pl.when(c == last)
    def _():
        lam = _lambda(lamv_ref)
        m0 = m_ref[...]
        k16 = jnp.concatenate([kn_ref[0], kn_ref[0]], axis=0)
        v16 = jnp.concatenate([vn_ref[0], vn_ref[0]], axis=0)
        s_new = jnp.sum(q16 * k16, axis=-1, keepdims=True) + bnew_ref[...]
        m1 = jnp.maximum(m0, s_new)
        a1 = jnp.exp(m0 - m1)
        e_new = jnp.exp(s_new - m1)
        l1 = a1 * l_ref[...] + e_new
        acc1 = a1 * acc_ref[...] + e_new * v16
        o = acc1[:ATT_HEADS] / l1[:ATT_HEADS] - lam * (acc1[ATT_HEADS:] / l1[ATT_HEADS:])
        ms = jnp.mean(o * o, axis=-1, keepdims=True)
        o_ref[0] = o * lax.rsqrt(ms + SUBLN_EPS) * (sw_ref[...] * (1.0 - LAM_INIT))


def attn_decode(q, k_new, v_new, cache_k, cache_v, page_table, lamv, rel_bias, subln_w):
    nb, n_pages = page_table.shape
    pages = next(g for g in (8, 4, 2) if n_pages % g == 0)
    n_chunks = n_pages // pages
    width = pages * PAGE
    bd0 = _bias_by_distance(rel_bias, width + 1)
    near = bd0[:, width - jnp.arange(width, dtype=I32)]
    bias_last = jnp.concatenate([near, near], axis=0)
    bnew = jnp.concatenate([bd0[:, 0:1], bd0[:, 0:1]], axis=0)
    row = lambda: pl.BlockSpec((1, ATT_HEADS, HEAD_W), lambda b, c, pt: (b, 0, 0))
    page = lambda g: pl.BlockSpec((1, PAGE * ATT_HEADS, HEAD_W), lambda b, c, pt: (pt[b, c * pages + g], 0, 0))
    const = lambda shape: pl.BlockSpec(shape, lambda b, c, pt: (0,) * len(shape))
    grid_spec = pltpu.PrefetchScalarGridSpec(
        num_scalar_prefetch=1,
        grid=(nb, n_chunks),
        in_specs=[const((4, ATT_DK)), row(), row(), row(), const((2 * ATT_HEADS, width)),
                  const((2 * ATT_HEADS, 1)), const((1, HEAD_W))]
                 + [page(g) for g in range(pages)] + [page(g) for g in range(pages)],
        out_specs=row(),
        scratch_shapes=[pltpu.VMEM((2 * ATT_HEADS, 1), F32), pltpu.VMEM((2 * ATT_HEADS, 1), F32),
                        pltpu.VMEM((2 * ATT_HEADS, HEAD_W), F32)],
    )
    return pl.pallas_call(
        functools.partial(_attn_decode_kernel, pages=pages),
        grid_spec=grid_spec,
        out_shape=jax.ShapeDtypeStruct((nb, ATT_HEADS, HEAD_W), F32),
        compiler_params=_cparams(("parallel", "arbitrary"), 56),
        name="attn_decode",
    )(page_table, lamv, q, k_new, v_new, bias_last, bnew, subln_w.reshape(1, HEAD_W),
      *([cache_k] * pages), *([cache_v] * pages))


def _stack_lane_blocks(x, width=LANES):
    return jnp.concatenate([x[:, c * width:(c + 1) * width] for c in range(x.shape[1] // width)], axis=0)


def _unstack_lane_blocks(y, n):
    return jnp.concatenate([y[c * n:(c + 1) * n] for c in range(y.shape[0] // n)], axis=1)


def _segsum(x, r2_ref):
    n = x.shape[0]
    xs = _stack_lane_blocks(x)
    hi = xs.astype(BF16)
    lo = (xs - hi.astype(F32)).astype(BF16)
    y = jnp.dot(jnp.concatenate([hi, lo], axis=1), r2_ref[...], preferred_element_type=F32)
    return _unstack_lane_blocks(y, n)


def _rwkv_prep_kernel(p_ref, sp_ref, mu_ref, w0_ref, w2_ref, a0_ref, a2_ref, g2_ref, kk_ref, ka_ref, rk_ref, r2_ref,
                      a_out, c_out, w_out, b_out, k_out, v_out, vkr_out, bon_out, g_out, prev_ref, *, tt):
    ti = pl.program_id(1)

    @pl.when(ti == 0)
    def _():
        prev_ref[...] = sp_ref[0]

    p = p_ref[0]
    prev_row = prev_ref[...]
    if tt > 1:
        rolled = pltpu.roll(p, 1, axis=0)
        row = lax.broadcasted_iota(I32, p.shape, 0)
        p_prev = jnp.where(row == 0, prev_row, rolled)
    else:
        p_prev = prev_row
    prev_ref[...] = p[tt - 1:tt, :]
    pm = p + mu_ref[...] * (p_prev - p)
    r = pm[:, 0:RW_W]
    kx = pm[:, RW_W:2 * RW_W]
    vx = pm[:, 2 * RW_W:3 * RW_W]
    lora = pm[:, 3 * RW_W:3 * RW_W + RW_LORA_W]
    gl = pm[:, 3 * RW_W + RW_LORA_W:RW_PROJ]
    dw = jnp.dot(jnp.tanh(lora).astype(BF16), w2_ref[...], preferred_element_type=F32)
    w = -jax.nn.softplus(-(w0_ref[...] + dw)) - 0.5
    decay = jnp.exp(-jnp.exp(w))
    a = jax.nn.sigmoid(a0_ref[...] + jnp.dot(lora.astype(BF16), a2_ref[...], preferred_element_type=F32))
    g = jnp.dot(jax.nn.sigmoid(gl).astype(BF16), g2_ref[...], preferred_element_type=F32)
    kk = kx * kk_ref[...]
    kkn = kk / jnp.maximum(jnp.sqrt(_segsum(kk * kk, r2_ref)), 1e-12)
    k = kx * (1.0 + (a - 1.0) * ka_ref[...])
    av = -kkn
    bv = kkn * a
    br = _segsum(bv * r, r2_ref)
    kr = _segsum(k * r, r2_ref)
    a_out[0] = av
    c_out[0] = decay * r + av * br
    w_out[0] = decay
    b_out[0] = bv
    k_out[0] = k
    v_out[0] = vx
    vkr_out[0] = vx * kr
    bon_out[0] = _segsum(r * k * rk_ref[...], r2_ref) * vx
    g_out[0] = g


def _value_diag():
    row = lax.broadcasted_iota(I32, (RW_N, RW_W), 0)
    lane = lax.broadcasted_iota(I32, (RW_N, RW_W), 1)
    return (lane & (RW_N - 1)) == row


def _scan_step(seq_refs, s_ref, t, e_j, zacc, rmat, diag, *, nb, v_terms):
    a_ref, c_ref, w_ref, b_ref, k_ref, v_ref = seq_refs
    width = rmat.shape[0]
    per = (RW_W // width) * RW_N
    row = lambda ref, bb: ref[bb, pl.ds(t, 1), :]
    nterm = 1 + v_terms
    stacked, pz = [], []
    for bb in range(nb):
        s = s_ref[bb]
        v = row(v_ref, bb)
        v_hi = jnp.where(diag, v, 0.0).astype(BF16)
        terms = [(s * row(a_ref, bb)).astype(BF16), v_hi]
        pz.append((s * row(c_ref, bb)).astype(BF16))
        if v_terms == 2:
            terms.append(jnp.where(diag, v - v_hi.astype(F32), 0.0).astype(BF16))
        stacked += [_stack_lane_blocks(x, width) for x in terms]
    red = jnp.dot(jnp.concatenate(stacked, axis=0), rmat, preferred_element_type=F32)
    for bb in range(nb):
        parts = [_unstack_lane_blocks(red[(nterm * bb + i) * per:(nterm * bb + i + 1) * per], RW_N)
                 for i in range(nterm)]
        vb = parts[1] if v_terms == 1 else parts[1] + parts[2]
        s_ref[bb] = s_ref[bb] * row(w_ref, bb) + parts[0] * row(b_ref, bb) + vb * row(k_ref, bb)
    return zacc + jnp.dot(jnp.concatenate(pz, axis=0), e_j, preferred_element_type=F32)


def _rwkv_scan_kernel(*refs, nb, tt, nst, v_terms):
    seq_refs = refs[:6]
    s0_ref, r_ref, e_ref, y_ref, so_ref, s_ref = refs[6:]
    ti = pl.program_id(1)

    @pl.when(ti == 0)
    def _():
        s_ref[...] = s0_ref[...]

    diag = _value_diag()
    rmat = r_ref[...]

    def group(g, carry):
        zacc = jnp.zeros((nb * RW_N, LANES), F32)
        for j in range(nst):
            zacc = _scan_step(seq_refs, s_ref, g * nst + j, e_ref[j], zacc, rmat, diag, nb=nb, v_terms=v_terms)
        for bb in range(nb):
            y_ref[bb, g] = zacc[bb * RW_N:(bb + 1) * RW_N]
        return carry

    lax.fori_loop(0, tt // nst, group, 0)

    @pl.when(ti == pl.num_programs(1) - 1)
    def _():
        so_ref[...] = s_ref[...]


def _rwkv_post_kernel(y_ref, vkr_ref, bon_ref, g_ref, lw_ref, lb_ref, r2_ref, o_ref):
    y = y_ref[...] + vkr_ref[...]
    mu = _segsum(y, r2_ref) * (1.0 / RW_N)
    d = y - mu
    var = _segsum(d * d, r2_ref) * (1.0 / RW_N)
    yn = d * lax.rsqrt(var + RW_LN_EPS) * lw_ref[...] + lb_ref[...]
    o_ref[...] = ((yn + bon_ref[...]) * g_ref[...]).astype(BF16)


def _segment_ones(width):
    return jnp.kron(jnp.eye(width // RW_N, dtype=F32), jnp.ones((RW_N, RW_N), F32)).astype(BF16)


def _segsum_operand():
    seg = _segment_ones(LANES)
    return jnp.concatenate([seg, seg], axis=0)


def rwkv_prep(prw, shift_prev, prm, batch, seq):
    tt = min(seq, 128)
    r2 = _segsum_operand()
    zeros64 = jnp.zeros((RW_N, RW_W), F32)
    w2p = jnp.concatenate([prm["rw_w2"], zeros64], axis=0).astype(BF16)
    a2p = jnp.concatenate([zeros64, prm["rw_a2"]], axis=0).astype(BF16)
    row = lambda a: a.reshape(1, -1)
    blk_in = pl.BlockSpec((1, tt, RW_PROJ), lambda b, i: (b, i, 0))
    blk_out = pl.BlockSpec((1, tt, RW_W), lambda b, i: (b, i, 0))
    c2 = lambda shape: pl.BlockSpec(shape, lambda b, i: (0,) * len(shape))
    return pl.pallas_call(
        functools.partial(_rwkv_prep_kernel, tt=tt),
        grid=(batch, seq // tt),
        in_specs=[blk_in, pl.BlockSpec((1, 1, RW_PROJ), lambda b, i: (b, 0, 0)), c2((1, RW_PROJ)),
                  c2((1, RW_W)), c2((RW_LORA_W, RW_W)), c2((1, RW_W)), c2((RW_LORA_W, RW_W)),
                  c2((RW_GATE_W, RW_W)), c2((1, RW_W)), c2((1, RW_W)), c2((1, RW_W)), c2((2 * LANES, LANES))],
        out_specs=[blk_out] * 9,
        out_shape=[jax.ShapeDtypeStruct((batch, seq, RW_W), F32)] * 9,
        scratch_shapes=[pltpu.VMEM((1, RW_PROJ), F32)],
        compiler_params=_cparams(("parallel", "arbitrary"), 56),
        name="rwkv_prep",
    )(prw.reshape(batch, seq, RW_PROJ), shift_prev.reshape(batch, 1, RW_PROJ), row(prm["rw_mu"]),
      row(prm["rw_w0"]), w2p, row(prm["rw_a0"]), a2p, prm["rw_g2"].astype(BF16), row(prm["rw_k_k"]),
      row(prm["rw_k_a"]), row(prm["rw_r_k"]), r2)


SCAN_NB = 4


def _scan_group(seq):
    return min(seq, LANES // RW_HEADS)


def _scan_operands(wkv0, batch, seq):
    nst = _scan_group(seq)
    head = jnp.arange(RW_W, dtype=I32) // RW_N
    ecomp = (head[None, :, None] + RW_HEADS * jnp.arange(nst, dtype=I32)[:, None, None]
             == jnp.arange(LANES, dtype=I32)[None, None, :]).astype(BF16)
    s0 = wkv0.astype(F32).transpose(0, 2, 1, 3).reshape(batch, RW_N, RW_W)
    return s0, _segment_ones(MXU_DIM), ecomp


def _scan_out_shapes(batch, seq):
    return [jax.ShapeDtypeStruct((batch, seq // _scan_group(seq), RW_N, LANES), F32),
            jax.ShapeDtypeStruct((batch, RW_N, RW_W), F32)]


def rwkv_scan(seq_arrays, wkv0, batch, seq):
    nst = _scan_group(seq)
    ts = min(seq, 64)
    nb = SCAN_NB
    v_terms = 2 if seq == 1 else 1
    c2 = lambda shape: pl.BlockSpec(shape, lambda b, i: (0,) * len(shape))
    seq_blk = pl.BlockSpec((nb, ts, RW_W), lambda b, i: (b, i, 0))
    grp_blk = pl.BlockSpec((nb, ts // nst, RW_N, LANES), lambda b, i: (b, i, 0, 0))
    st_blk = pl.BlockSpec((nb, RW_N, RW_W), lambda b, i: (b, 0, 0))
    return pl.pallas_call(
        functools.partial(_rwkv_scan_kernel, nb=nb, tt=ts, nst=nst, v_terms=v_terms),
        grid=(batch // nb, seq // ts),
        in_specs=[seq_blk] * 6 + [st_blk, c2((MXU_DIM, MXU_DIM)), c2((nst, RW_W, LANES))],
        out_specs=[grp_blk, st_blk],
        out_shape=_scan_out_shapes(batch, seq),
        scratch_shapes=[pltpu.VMEM((nb, RW_N, RW_W), F32)],
        compiler_params=_cparams(("parallel", "arbitrary"), 48),
        name="rwkv_scan",
    )(*seq_arrays, *_scan_operands(wkv0, batch, seq))


def rwkv_finish(yc, s_fin, vkr, bon, gv, prm, batch, seq):
    m = batch * seq
    nst = _scan_group(seq)
    row = lambda a: a.reshape(1, -1)
    yz = yc.reshape(batch, seq // nst, RW_N, LANES // RW_HEADS, RW_HEADS)[:, :, :, :nst, :]
    yz = yz.transpose(0, 1, 3, 4, 2).reshape(m, RW_W)
    wkv_new = s_fin.reshape(batch, RW_N, RW_HEADS, RW_N).transpose(0, 2, 1, 3)

    tm = min(m, ROW_TILE)
    rows = pl.BlockSpec((tm, RW_W), lambda i: (i, 0))
    y = pl.pallas_call(
        _rwkv_post_kernel,
        grid=(m // tm,),
        in_specs=[rows] * 4 + [_full((1, RW_W)), _full((1, RW_W)), _full((2 * LANES, LANES))],
        out_specs=rows,
        out_shape=jax.ShapeDtypeStruct((m, RW_W), BF16),
        compiler_params=_cparams(("parallel",), 48),
        name="rwkv_post",
    )(yz, vkr.reshape(m, RW_W), bon.reshape(m, RW_W), gv.reshape(m, RW_W), row(prm["rw_ln_w"]),
      row(prm["rw_ln_b"]), _segsum_operand())
    return y, wkv_new


def rwkv_mix(prw, shift_prev, wkv0, prm, batch, seq):
    *seq_arrays, vkr, bon, gv = rwkv_prep(prw, shift_prev, prm, batch, seq)
    yc, s_fin = rwkv_scan(seq_arrays[:6], wkv0, batch, seq)
    return rwkv_finish(yc, s_fin, vkr, bon, gv, prm, batch, seq)


def _merge_kernel(ya_ref, yb_ref, ga_ref, gb_ref, ba_ref, bb_ref, wa_ref, wb_ref, o_ref):
    pa = jnp.dot(ya_ref[...].astype(BF16), wa_ref[...], preferred_element_type=F32)
    pb = jnp.dot(yb_ref[...], wb_ref[...], preferred_element_type=F32)
    ga = jax.nn.sigmoid(ga_ref[...] + ba_ref[...])
    gb = jax.nn.sigmoid(gb_ref[...] + bb_ref[...])
    o_ref[...] = (ga * pa + gb * pb).astype(BF16)


def merge_branches(y_att, y_rw, g_lin, b_gate, wa, wb):
    m = y_att.shape[0]
    tm = min(m, ROW_TILE)
    d = D_MODEL
    return pl.pallas_call(
        _merge_kernel,
        grid=(m // tm,),
        in_specs=[pl.BlockSpec((tm, ATT_W), lambda i: (i, 0)), pl.BlockSpec((tm, RW_W), lambda i: (i, 0)),
                  pl.BlockSpec((tm, d), lambda i: (i, 0)), pl.BlockSpec((tm, d), lambda i: (i, 1)),
                  pl.BlockSpec((1, d), lambda i: (0, 0)), pl.BlockSpec((1, d), lambda i: (0, 1)),
                  _full((ATT_W, d)), _full((RW_W, d))],
        out_specs=pl.BlockSpec((tm, d), lambda i: (i, 0)),
        out_shape=jax.ShapeDtypeStruct((m, d), BF16),
        compiler_params=_cparams(("parallel",), 48),
        name="merge_branches",
    )(y_att, y_rw, g_lin, g_lin, b_gate.reshape(1, 2 * d), b_gate.reshape(1, 2 * d), wa, wb)


SLAB = D_MODEL // LANES


def _store_token_slabs(ref, x):
    n = x.shape[0]
    for c in range(SLAB):
        ref[pl.ds(c, n, stride=SLAB), :] = x[:, c * LANES:(c + 1) * LANES]


def _load_token_slabs(ref, n):
    return jnp.concatenate([ref[pl.ds(c, n, stride=SLAB), :] for c in range(SLAB)], axis=1)


def _outproj_router_kernel(mg_ref, wo_ref, x_ref, nw_ref, wr_ref, br_ref, h_ref, hn_ref, eid_ref, ew_ref):
    h = x_ref[...] + jnp.dot(mg_ref[...], wo_ref[...], preferred_element_type=F32)
    h_ref[...] = h
    ms = jnp.mean(h * h, axis=-1, keepdims=True)
    hn = h * lax.rsqrt(ms + NORM_EPS) * nw_ref[...]
    _store_token_slabs(hn_ref, hn)
    lg = jnp.dot(hn.astype(BF16), wr_ref[...], preferred_element_type=F32) + br_ref[...]
    lane = lax.broadcasted_iota(I32, lg.shape, 1)
    lane_f = lane.astype(F32)
    big = float(LANES)
    lgm = jnp.where(lane < N_GROUPS, lg, NEG)
    gmax = jnp.max(lgm, axis=-1, keepdims=True)
    gidx = jnp.min(jnp.where(lgm == gmax, lane_f, big), axis=-1, keepdims=True)
    g_w = 1.0 / jnp.sum(jnp.exp(lgm - gmax), axis=-1, keepdims=True)
    lane_group = ((lane >> 3) - 1).astype(F32)
    in_group = (lane >= N_GROUPS) & (lane < N_GROUPS + N_EXPERTS) & (lane_group == gidx)
    le = jnp.where(in_group, lg, NEG)
    m1 = jnp.max(le, axis=-1, keepdims=True)
    i1 = jnp.min(jnp.where(le == m1, lane_f, big), axis=-1, keepdims=True)
    le2 = jnp.where(lane_f == i1, NEG, le)
    m2 = jnp.max(le2, axis=-1, keepdims=True)
    i2 = jnp.min(jnp.where(le2 == m2, lane_f, big), axis=-1, keepdims=True)
    e2 = jnp.exp(m2 - m1)
    w1 = g_w / (1.0 + e2)
    w2 = g_w * e2 / (1.0 + e2)
    col = lax.broadcasted_iota(I32, eid_ref.shape, 1)
    eid_ref[...] = jnp.where(col == 0, i1, i2).astype(I32) - N_GROUPS
    ew_ref[...] = jnp.where(col == 0, w1, w2)


def outproj_router(merged, w_out, x, norm_w, w_router, b_router):
    m, d = x.shape
    tm = min(m, ROW_TILE)
    rows = lambda w: pl.BlockSpec((tm, w), lambda i: (i, 0))
    return pl.pallas_call(
        _outproj_router_kernel,
        grid=(m // tm,),
        in_specs=[rows(d), _full((d, d)), rows(d), _full((1, d)), _full((d, LANES)), _full((1, LANES))],
        out_specs=[rows(d), pl.BlockSpec((tm * SLAB, LANES), lambda i: (i, 0)), rows(2), rows(2)],
        out_shape=[jax.ShapeDtypeStruct((m, d), F32), jax.ShapeDtypeStruct((m * SLAB, LANES), F32),
                   jax.ShapeDtypeStruct((m, 2), I32), jax.ShapeDtypeStruct((m, 2), F32)],
        compiler_params=_cparams(("parallel",), 48),
        name="outproj_router",
    )(merged, w_out, x, norm_w.reshape(1, d), w_router, b_router)


def _moe_kernel(starts_ref, counts_ref, pcounts_ref, src_ref, hp_ref, hs_ref, wg_ref, wu_ref, wd_ref,
                yp_ref, ys_ref, wgb, wub, wdb, xbuf, ybuf, sem_in, sem_out, *, n_prompt, n_sample):
    e = pl.program_id(0)

    @pl.when(e == 0)
    def _():
        xbuf[...] = jnp.zeros(xbuf.shape, F32)

    n = counts_ref[e]
    n_p = pcounts_ref[e]
    start = starts_ref[e]
    nblk = (n + MOE_BLK - 1) // MOE_BLK

    def slab(i):
        return pl.ds(pl.multiple_of(i * SLAB, SLAB), SLAB)

    def row_in(src_hbm, tok, slot, r):
        return pltpu.make_async_copy(src_hbm.at[slab(tok)], xbuf.at[slot, slab(r)], sem_in.at[slot])

    def row_out(dst_hbm, dst, r):
        return pltpu.make_async_copy(ybuf.at[slab(r)], dst_hbm.at[slab(dst)], sem_out)

    def for_rows(lo, hi, body):
        whole = (lo == 0) & (hi == MOE_BLK)

        @pl.when(whole)
        def _():
            def chunk(i, c):
                for u in range(MOE_UNROLL):
                    body(i * MOE_UNROLL + u)
                return c
            lax.fori_loop(0, MOE_BLK // MOE_UNROLL, chunk, 0)

        @pl.when(jnp.logical_not(whole))
        def _():
            lax.fori_loop(lo, hi, lambda r, c: (body(r), c)[1], 0)

    def block_rows(bi):
        nv = jnp.minimum(MOE_BLK, n - bi * MOE_BLK)
        nvp = jnp.clip(n_p - bi * MOE_BLK, 0, nv)
        return start + bi * MOE_BLK, nv, nvp

    def gather_start(bi, slot):
        r0, nv, nvp = block_rows(bi)
        for_rows(0, nvp, lambda r: row_in(hp_ref, src_ref[r0 + r] >> 1, slot, r).start(priority=ROW_DMA_PRIORITY))
        for_rows(nvp, nv, lambda r: row_in(hs_ref, (src_ref[r0 + r] >> 1) - n_prompt, slot, r)
                 .start(priority=ROW_DMA_PRIORITY))

    def gather_wait(bi, slot):
        _, nv, _ = block_rows(bi)
        for_rows(0, nv, lambda r: row_in(hp_ref, 0, slot, r).wait())

    def scatter_start(bi):
        r0, nv, nvp = block_rows(bi)

        def to_prompt(r):
            src = src_ref[r0 + r]
            row_out(yp_ref, (src & 1) * n_prompt + (src >> 1), r).start(priority=ROW_DMA_PRIORITY)

        def to_sample(r):
            src = src_ref[r0 + r]
            row_out(ys_ref, (src & 1) * n_sample + (src >> 1) - n_prompt, r).start(priority=ROW_DMA_PRIORITY)

        for_rows(0, nvp, to_prompt)
        for_rows(nvp, nv, to_sample)

    def scatter_wait(bi):
        _, nv, _ = block_rows(bi)
        for_rows(0, nv, lambda r: row_out(yp_ref, 0, r).wait())

    @pl.when(n > 0)
    def _():
        gather_start(0, 0)
        wgb[...] = wg_ref[0].astype(BF16)
        wub[...] = wu_ref[0].astype(BF16)
        wdb[...] = wd_ref[0].astype(BF16)

        def block(bi, carry):
            slot = bi & 1
            gather_wait(bi, slot)

            @pl.when(bi + 1 < nblk)
            def _():
                gather_start(bi + 1, 1 - slot)

            x = _load_token_slabs(xbuf.at[slot], MOE_BLK).astype(BF16)
            hg = jnp.dot(x, wgb[...], preferred_element_type=F32)
            hu = jnp.dot(x, wub[...], preferred_element_type=F32)
            hd = (hg * jax.nn.sigmoid(hg) * hu).astype(BF16)
            y = jnp.dot(hd, wdb[...], preferred_element_type=F32)

            @pl.when(bi > 0)
            def _():
                scatter_wait(bi - 1)

            _store_token_slabs(ybuf, y)
            scatter_start(bi)
            return carry

        lax.fori_loop(0, nblk, block, 0)
        scatter_wait(nblk - 1)


def moe_experts(hn_p, hn_s, eid, w_gate, w_up, w_down):
    d = D_MODEL
    n_prompt = hn_p.shape[0] // SLAB
    n_sample = hn_s.shape[0] // SLAB
    n_rows = eid.shape[0] * 2
    flat_e = eid.reshape(n_rows)
    onehot = (flat_e[:, None] == jnp.arange(N_EXPERTS, dtype=I32)[None, :]).astype(I32)
    counts = jnp.sum(onehot, axis=0)
    pcounts = jnp.sum(onehot[:2 * n_prompt], axis=0)
    starts = jnp.cumsum(counts) - counts
    rank = jnp.sum(onehot * (jnp.cumsum(onehot, axis=0) - 1), axis=1)
    pos = starts[flat_e] + rank
    src = jnp.zeros((n_rows,), I32).at[pos].set(jnp.arange(n_rows, dtype=I32), unique_indices=True,
                                                mode="promise_in_bounds")
    any_spec = pl.BlockSpec(memory_space=pl.ANY)
    grid_spec = pltpu.PrefetchScalarGridSpec(
        num_scalar_prefetch=4,
        grid=(N_EXPERTS,),
        in_specs=[any_spec, any_spec,
                  pl.BlockSpec((1, d, D_FF), lambda e, *_: (e, 0, 0)),
                  pl.BlockSpec((1, d, D_FF), lambda e, *_: (e, 0, 0)),
                  pl.BlockSpec((1, D_FF, d), lambda e, *_: (e, 0, 0))],
        out_specs=[any_spec, any_spec],
        scratch_shapes=[pltpu.VMEM((d, D_FF), BF16), pltpu.VMEM((d, D_FF), BF16), pltpu.VMEM((D_FF, d), BF16),
                        pltpu.VMEM((2, MOE_BLK * SLAB, LANES), F32), pltpu.VMEM((MOE_BLK * SLAB, LANES), F32),
                        pltpu.SemaphoreType.DMA((2,)), pltpu.SemaphoreType.DMA(())],
    )
    return pl.pallas_call(
        functools.partial(_moe_kernel, n_prompt=n_prompt, n_sample=n_sample),
        grid_spec=grid_spec,
        out_shape=[jax.ShapeDtypeStruct((2 * n_prompt * SLAB, LANES), F32),
                   jax.ShapeDtypeStruct((2 * n_sample * SLAB, LANES), F32)],
        compiler_params=_cparams(("arbitrary",), 56),
        name="moe_experts",
    )(starts, counts, pcounts, src, hn_p, hn_s, w_gate, w_up, w_down)


def _final_kernel(h_ref, y0_ref, y1_ref, ew_ref, nw_ref, o_ref):
    ew = ew_ref[...]
    tm = h_ref.shape[0]
    x = (h_ref[...] + ew[:, 0:1] * _load_token_slabs(y0_ref, tm) + ew[:, 1:2] * _load_token_slabs(y1_ref, tm))
    ms = jnp.mean(x * x, axis=-1, keepdims=True)
    o_ref[...] = x * lax.rsqrt(ms + NORM_EPS) * nw_ref[...]


def combine_final(h, y_rows, ew, norm_w):
    m, d = h.shape
    tm = min(m, ROW_TILE)
    o1 = m // tm
    return pl.pallas_call(
        _final_kernel,
        grid=(m // tm,),
        in_specs=[pl.BlockSpec((tm, d), lambda i: (i, 0)),
                  pl.BlockSpec((tm * SLAB, LANES), lambda i: (i, 0)),
                  pl.BlockSpec((tm * SLAB, LANES), lambda i: (i + o1, 0)),
                  pl.BlockSpec((tm, 2), lambda i: (i, 0)), _full((1, d))],
        out_specs=pl.BlockSpec((tm, d), lambda i: (i, 0)),
        out_shape=jax.ShapeDtypeStruct((m, d), F32),
        compiler_params=_cparams(("parallel",), 48),
        name="combine_final",
    )(h, y_rows, y_rows, ew, norm_w.reshape(1, d))


_RW_COL0 = 3 * ATT_W
_GATE_COL0 = _RW_COL0 + RW_PROJ


def _in_proj(x2, prm, q_dtype):
    xn = rmsnorm_bf16(x2, prm["norm_mix_w"])
    w_in = prm["w_in"]
    q = matmul_cols(xn, w_in, 0, ATT_W, q_dtype)
    k = matmul_cols(xn, w_in, ATT_W, ATT_W, F32)
    v = matmul_cols(xn, w_in, 2 * ATT_W, ATT_W, F32)
    prw = matmul_cols(xn, w_in, _RW_COL0, RW_PROJ, F32)
    g_lin = matmul_cols(xn, w_in, _GATE_COL0, 2 * D_MODEL, F32)
    return q, k, v, prw, g_lin


def _mixer_tail(x2, y_att, y_rw, g_lin, prm):
    merged = merge_branches(y_att, y_rw, g_lin, prm["b_gate"], prm["wa_bf16"], prm["wb_bf16"])
    return outproj_router(merged, prm["wo_bf16"], x2, prm["norm_ffn_w"], prm["w_router"], prm["b_router"])


def kernel(x_prompt, x_sample, cache_k, cache_v, state_wkv, state_shift, page_table, norm_mix_w, w_in, b_gate, lambda_q1, lambda_k1, lambda_q2, lambda_k2, subln_w, rel_bias, rw_mu, rw_w0, rw_w2, rw_a0, rw_a2, rw_g2, rw_k_k, rw_k_a, rw_r_k, rw_ln_w, rw_ln_b, w_branch_a, w_branch_b, w_out, norm_ffn_w, w_router_g, b_router_g, w_router_e, b_router_e, w_gate_e, w_up_e, w_down_e, norm_final_w):
    assert w_in.shape[0] == 1, "single-layer trunk"
    bp, sp, d = x_prompt.shape
    bs, ss, _ = x_sample.shape
    assert ss == 1 and d == D_MODEL
    n_pool = cache_k.shape[1]
    pad_lanes = LANES - N_GROUPS - N_EXPERTS
    prm = {
        "norm_mix_w": norm_mix_w[0], "w_in": w_in[0], "b_gate": b_gate[0],
        "rw_mu": rw_mu[0], "rw_w0": rw_w0[0], "rw_w2": rw_w2[0], "rw_a0": rw_a0[0], "rw_a2": rw_a2[0],
        "rw_g2": rw_g2[0], "rw_k_k": rw_k_k[0], "rw_k_a": rw_k_a[0], "rw_r_k": rw_r_k[0].reshape(RW_W),
        "rw_ln_w": rw_ln_w[0], "rw_ln_b": rw_ln_b[0],
        "wa_bf16": w_branch_a[0].astype(BF16), "wb_bf16": w_branch_b[0].astype(BF16),
        "wo_bf16": w_out[0].astype(BF16), "norm_ffn_w": norm_ffn_w[0],
        "w_router": jnp.concatenate([w_router_g[0], w_router_e[0], jnp.zeros((d, pad_lanes), F32)],
                                    axis=1).astype(BF16),
        "b_router": jnp.concatenate([b_router_g[0], b_router_e[0], jnp.zeros((pad_lanes,), F32)]).reshape(1, LANES),
    }
    lamv = jnp.stack([lambda_q1[0], lambda_k1[0], lambda_q2[0], lambda_k2[0]]).astype(F32)
    sub_w = subln_w[0]

    mp = bp * sp
    xp2 = x_prompt.reshape(mp, d)
    q_p, k_p, v_p, prw_p, g_p = _in_proj(xp2, prm, BF16)
    att_p = attn_prompt(q_p, k_p, v_p, lamv, rel_bias, sub_w, bp, sp)
    rw_p, wkv_p = rwkv_mix(prw_p, jnp.zeros((bp, RW_PROJ), F32), jnp.zeros((bp, RW_HEADS, RW_N, RW_N), F32),
                           prm, bp, sp)
    h_p, hn_p, eid_p, ew_p = _mixer_tail(xp2, att_p, rw_p, g_p, prm)

    xs2 = x_sample.reshape(bs, d)
    q_s, k_s, v_s, prw_s, g_s = _in_proj(xs2, prm, F32)
    heads = lambda a: a.reshape(bs, ATT_HEADS, HEAD_W)
    att_s = attn_decode(heads(q_s), heads(k_s), heads(v_s),
                        cache_k[0].reshape(n_pool, PAGE * ATT_HEADS, HEAD_W),
                        cache_v[0].reshape(n_pool, PAGE * ATT_HEADS, HEAD_W),
                        page_table, lamv, rel_bias, sub_w).reshape(bs, ATT_W)
    rw_s, wkv_s = rwkv_mix(prw_s, state_shift[0], state_wkv[0], prm, bs, 1)
    h_s, hn_s, eid_s, ew_s = _mixer_tail(xs2, att_s, rw_s, g_s, prm)

    yr_p, yr_s = moe_experts(hn_p, hn_s, jnp.concatenate([eid_p, eid_s], axis=0),
                             w_gate_e[0], w_up_e[0], w_down_e[0])
    y_p = combine_final(h_p, yr_p, ew_p, norm_final_w)
    y_s = combine_final(h_s, yr_s, ew_s, norm_final_w)

    return (y_p.reshape(bp, sp, d), y_s.reshape(bs, 1, d),
            k_p.reshape(1, bp, sp, ATT_HEADS, HEAD_W), v_p.reshape(1, bp, sp, ATT_HEADS, HEAD_W),
            wkv_p.astype(state_wkv.dtype)[None], prw_p.reshape(bp, sp, RW_PROJ)[None, :, -1],
            k_s.reshape(1, bs, 1, ATT_HEADS, HEAD_W), v_s.reshape(1, bs, 1, ATT_HEADS, HEAD_W),
            wkv_s.astype(state_wkv.dtype)[None], prw_s.reshape(1, bs, RW_PROJ))
```

```python
import functools
import math

import jax
import jax.numpy as jnp
from jax import lax
from jax.experimental import pallas as pl
from jax.experimental.pallas import tpu as pltpu

F32 = jnp.float32
BF16 = jnp.bfloat16
I32 = jnp.int32

D_MODEL = 2048
ATT_HEADS = 8
ATT_DK = 64
HEAD_W = 2 * ATT_DK
ATT_W = ATT_HEADS * HEAD_W
ATT_SCALE = ATT_DK ** -0.5
RW_HEADS = 16
RW_N = 64
RW_W = RW_HEADS * RW_N
RW_LORA_W = 128
RW_GATE_W = 128
RW_PROJ = 3 * RW_W + RW_LORA_W + RW_GATE_W
N_BUCKETS = 32
MAX_DISTANCE = 128
PAGE = 128
EXPERTS_PER_GROUP = 8
N_GROUPS = 8
N_EXPERTS = N_GROUPS * EXPERTS_PER_GROUP
D_FF = 512
NORM_EPS = 1e-6
SUBLN_EPS = 1e-5
RW_LN_EPS = 64e-5
LAM_INIT = 0.8 - 0.6 * math.exp(-0.3 * 0)
NEG = -1e30
LANES = 128
MXU_DIM = 256
MOE_BLK = 128
MOE_UNROLL = 16
ROW_TILE = 256
MIB = 1024 * 1024


def _cparams(sem, vmem_mib):
    return pltpu.CompilerParams(dimension_semantics=sem, vmem_limit_bytes=vmem_mib * MIB)


def _full(shape):
    nd = len(shape)
    return pl.BlockSpec(shape, lambda *_: (0,) * nd)


def _rmsnorm_bf16_kernel(x_ref, w_ref, o_ref):
    x = x_ref[...]
    ms = jnp.mean(x * x, axis=-1, keepdims=True)
    o_ref[...] = (x * lax.rsqrt(ms + NORM_EPS) * w_ref[...]).astype(BF16)


def rmsnorm_bf16(x, w):
    m, d = x.shape
    tm = min(m, 512)
    return pl.pallas_call(
        _rmsnorm_bf16_kernel,
        grid=(m // tm,),
        in_specs=[pl.BlockSpec((tm, d), lambda i: (i, 0)), _full((1, d))],
        out_specs=pl.BlockSpec((tm, d), lambda i: (i, 0)),
        out_shape=jax.ShapeDtypeStruct((m, d), BF16),
        compiler_params=_cparams(("parallel",), 32),
        name="rmsnorm_bf16",
    )(x, w.reshape(1, d))


def _matmul_kernel(x_ref, w_ref, o_ref):
    o_ref[...] = jnp.dot(x_ref[...], w_ref[...].astype(BF16),
                         preferred_element_type=F32).astype(o_ref.dtype)


def matmul_cols(x, w, col0, ncols, out_dtype):
    m, k = x.shape
    tm = next(t for t in (2048, 1024, 512, 256, m) if m % t == 0)
    tn = next(t for t in (512, 256, 128) if col0 % t == 0 and ncols % t == 0)
    off = col0 // tn
    return pl.pallas_call(
        _matmul_kernel,
        grid=(m // tm, ncols // tn),
        in_specs=[pl.BlockSpec((tm, k), lambda i, j: (i, 0)),
                  pl.BlockSpec((k, tn), lambda i, j: (0, j + off))],
        out_specs=pl.BlockSpec((tm, tn), lambda i, j: (i, j)),
        out_shape=jax.ShapeDtypeStruct((m, ncols), out_dtype),
        compiler_params=_cparams(("parallel", "arbitrary"), 48),
        name="in_proj",
    )(x, w)


def _rel_bucket(dist):
    n = jnp.maximum(dist, 0)
    max_exact = N_BUCKETS // 2
    large = max_exact + (jnp.log(jnp.maximum(n, 1).astype(F32) / max_exact)
                         / math.log(MAX_DISTANCE / max_exact) * (N_BUCKETS - max_exact)).astype(I32)
    large = jnp.minimum(large, N_BUCKETS - 1)
    return jnp.where(n < max_exact, n, large)


def _bias_by_distance(rel_bias, n):
    d = jnp.arange(n, dtype=I32)
    table = rel_bias.astype(F32)[_rel_bucket(d)]
    far = rel_bias.astype(F32)[N_BUCKETS - 1]
    return (table - far[None, :]).T


def _toeplitz(vals, t):
    h = vals.shape[0]
    w = jnp.concatenate([vals[:, ::-1], jnp.zeros((h, 1), vals.dtype)], axis=1)
    flat = jnp.tile(w, (1, t))[:, :t * (2 * t - 1)]
    return flat.reshape(h, t, 2 * t - 1)[:, :, t - 1:]


def _lambda(lamv_ref):
    lv = lamv_ref[...]
    s1 = jnp.sum(lv[0:1] * lv[1:2], axis=-1, keepdims=True)
    s2 = jnp.sum(lv[2:3] * lv[3:4], axis=-1, keepdims=True)
    return jnp.exp(s1) - jnp.exp(s2) + LAM_INIT


def _split_halves(q):
    lane = lax.broadcasted_iota(I32, q.shape, 1)
    return jnp.concatenate([jnp.where(lane < ATT_DK, q, 0.0), jnp.where(lane >= ATT_DK, q, 0.0)], axis=0)


def _attn_prompt_kernel(lamv_ref, q_ref, k_ref, v_ref, bd_ref, bp_ref, sw_ref, o_ref, kb_ref, v1_ref, *, t, nq):
    kb_ref[...] = k_ref[...].astype(BF16)
    v1_ref[:, :HEAD_W] = v_ref[...].astype(BF16)
    v1_ref[:, HEAD_W:] = jnp.ones((v1_ref.shape[0], HEAD_W), BF16)
    lam = _lambda(lamv_ref)
    sw = sw_ref[...] * (1.0 - LAM_INIT)
    bd = bd_ref[0]
    bp = bp_ref[0]
    bd2 = jnp.concatenate([bd, bd], axis=0)
    bp2 = jnp.concatenate([bp, bp], axis=0)

    for i in range(nq):
        n_keys = (i + 1) * t
        q = q_ref[i * t:(i + 1) * t, :].astype(F32) * ATT_SCALE
        q2 = _split_halves(q).astype(BF16)
        s = lax.dot_general(q2, kb_ref[:n_keys, :], (((1,), (1,)), ((), ())), preferred_element_type=F32)
        near = [s[:, i * t:] + bd2]
        if i >= 1:
            near.insert(0, s[:, (i - 1) * t:i * t] + bp2)
        s = jnp.concatenate(([s[:, :(i - 1) * t]] if i >= 2 else []) + near, axis=1)
        e = jnp.exp(s - jnp.max(s, axis=-1, keepdims=True)).astype(BF16)
        a = jnp.dot(e, v1_ref[:n_keys, :], preferred_element_type=F32)
        a = a[:, :HEAD_W] / a[:, HEAD_W:]
        o = a[:t] - lam * a[t:]
        ms = jnp.mean(o * o, axis=-1, keepdims=True)
        o_ref[i * t:(i + 1) * t, :] = o * lax.rsqrt(ms + SUBLN_EPS) * sw


def attn_prompt(q, k, v, lamv, rel_bias, subln_w, batch, seq):
    t = min(seq, 256)
    nq = seq // t
    assert t >= LANES, "blocks two or more away must lie beyond the last distinct distance bucket"
    bd0 = _bias_by_distance(rel_bias, 2 * t)
    bdiag = _toeplitz(jnp.concatenate([jnp.full((ATT_HEADS, t - 1), NEG, F32), bd0[:, :t]], axis=1), t)
    bprev = _toeplitz(bd0[:, 1:], t)
    blk = lambda: pl.BlockSpec((seq, HEAD_W), lambda b, h: (b, h))
    return pl.pallas_call(
        functools.partial(_attn_prompt_kernel, t=t, nq=nq),
        grid=(batch, ATT_HEADS),
        in_specs=[_full((4, ATT_DK)), blk(), blk(), blk(),
                  pl.BlockSpec((1, t, t), lambda b, h: (h, 0, 0)),
                  pl.BlockSpec((1, t, t), lambda b, h: (h, 0, 0)),
                  _full((1, HEAD_W))],
        out_specs=blk(),
        out_shape=jax.ShapeDtypeStruct((batch * seq, ATT_W), F32),
        scratch_shapes=[pltpu.VMEM((seq, HEAD_W), BF16), pltpu.VMEM((seq, 2 * HEAD_W), BF16)],
        compiler_params=_cparams(("parallel", "parallel"), 48),
        name="attn_prompt",
    )(lamv, q, k, v, bdiag, bprev, subln_w.reshape(1, HEAD_W))


def _attn_decode_kernel(pt_ref, lamv_ref, q_ref, kn_ref, vn_ref, bias_ref, bnew_ref, sw_ref, *rest, pages):
    del pt_ref
    k_refs = rest[:pages]
    v_refs = rest[pages:2 * pages]
    o_ref, m_ref, l_ref, acc_ref = rest[2 * pages:]
    c = pl.program_id(1)
    last = pl.num_programs(1) - 1

    @pl.when(c == 0)
    def _():
        m_ref[...] = jnp.full(m_ref.shape, NEG, F32)
        l_ref[...] = jnp.zeros(l_ref.shape, F32)
        acc_ref[...] = jnp.zeros(acc_ref.shape, F32)

    q16 = _split_halves(q_ref[0] * ATT_SCALE)
    row_head = lax.broadcasted_iota(I32, (2 * ATT_HEADS, 1), 0) & (ATT_HEADS - 1)
    qh = [jnp.where(row_head == h, q16, 0.0).astype(BF16) for h in range(ATT_HEADS)]
    pairs = pages // 2
    st = {"s": [None] * pairs, "pv": jnp.zeros((2 * ATT_HEADS, HEAD_W), F32)}

    def head_rows(ref, h):
        return ref[0, pl.ds(h, PAGE, stride=ATT_HEADS), :]

    def two_pages(refs, pp, h):
        return jnp.concatenate([head_rows(refs[2 * pp], h), head_rows(refs[2 * pp + 1], h)], axis=0).astype(BF16)

    def logits(pp):
        sp = None
        for h in range(ATT_HEADS):
            d = lax.dot_general(qh[h], two_pages(k_refs, pp, h), (((1,), (1,)), ((), ())),
                                preferred_element_type=F32)
            sp = d if sp is None else sp + d
        st["s"][pp] = sp

    def softmax_update():
        s = jnp.concatenate(st["s"], axis=1)
        s = s + bias_ref[...] * (c == last).astype(F32)
        m = m_ref[...]
        mn = jnp.maximum(m, jnp.max(s, axis=-1, keepdims=True))
        st["alpha"] = jnp.exp(m - mn)
        e = jnp.exp(s - mn)
        l_ref[...] = st["alpha"] * l_ref[...] + jnp.sum(e, axis=-1, keepdims=True)
        m_ref[...] = mn
        st["e"] = e.astype(BF16)

    def weighted_values(pp):
        ep = st["e"][:, pp * 2 * PAGE:(pp + 1) * 2 * PAGE]
        for h in range(ATT_HEADS):
            r = jnp.dot(ep, two_pages(v_refs, pp, h), preferred_element_type=F32)
            st["pv"] = st["pv"] + jnp.where(row_head == h, r, 0.0)

    def accumulate():
        acc_ref[...] = st["alpha"] * acc_ref[...] + st["pv"]

    for pp in range(pairs):
        logits(pp)
    softmax_update()
    for pp in range(pairs):
        weighted_values(pp)
    accumulate()

    @pl.when(c == last)
    def _():
        lam = _lambda(lamv_ref)
        m0 = m_ref[...]
        k16 = jnp.concatenate([kn_ref[0], kn_ref[0]], axis=0)
        v16 = jnp.concatenate([vn_ref[0], vn_ref[0]], axis=0)
        s_new = jnp.sum(q16 * k16, axis=-1, keepdims=True) + bnew_ref[...]
        m1 = jnp.maximum(m0, s_new)
        a1 = jnp.exp(m0 - m1)
        e_new = jnp.exp(s_new - m1)
        l1 = a1 * l_ref[...] + e_new
        acc1 = a1 * acc_ref[...] + e_new * v16
        o = acc1[:ATT_HEADS] / l1[:ATT_HEADS] - lam * (acc1[ATT_HEADS:] / l1[ATT_HEADS:])
        ms = jnp.mean(o * o, axis=-1, keepdims=True)
        o_ref[0] = o * lax.rsqrt(ms + SUBLN_EPS) * (sw_ref[...] * (1.0 - LAM_INIT))


def attn_decode(q, k_new, v_new, cache_k, cache_v, page_table, lamv, rel_bias, subln_w):
    nb, n_pages = page_table.shape
    pages = next(g for g in (8, 4, 2) if n_pages % g == 0)
    n_chunks = n_pages // pages
    width = pages * PAGE
    bd0 = _bias_by_distance(rel_bias, width + 1)
    near = bd0[:, width - jnp.arange(width, dtype=I32)]
    bias_last = jnp.concatenate([near, near], axis=0)
    bnew = jnp.concatenate([bd0[:, 0:1], bd0[:, 0:1]], axis=0)
    row = lambda: pl.BlockSpec((1, ATT_HEADS, HEAD_W), lambda b, c, pt: (b, 0, 0))
    page = lambda g: pl.BlockSpec((1, PAGE * ATT_HEADS, HEAD_W), lambda b, c, pt: (pt[b, c * pages + g], 0, 0))
    const = lambda shape: pl.BlockSpec(shape, lambda b, c, pt: (0,) * len(shape))
    grid_spec = pltpu.PrefetchScalarGridSpec(
        num_scalar_prefetch=1,
        grid=(nb, n_chunks),
        in_specs=[const((4, ATT_DK)), row(), row(), row(), const((2 * ATT_HEADS, width)),
                  const((2 * ATT_HEADS, 1)), const((1, HEAD_W))]
                 + [page(g) for g in range(pages)] + [page(g) for g in range(pages)],
        out_specs=row(),
        scratch_shapes=[pltpu.VMEM((2 * ATT_HEADS, 1), F32), pltpu.VMEM((2 * ATT_HEADS, 1), F32),
                        pltpu.VMEM((2 * ATT_HEADS, HEAD_W), F32)],
    )
    return pl.pallas_call(
        functools.partial(_attn_decode_kernel, pages=pages),
        grid_spec=grid_spec,
        out_shape=jax.ShapeDtypeStruct((nb, ATT_HEADS, HEAD_W), F32),
        compiler_params=_cparams(("parallel", "arbitrary"), 56),
        name="attn_decode",
    )(page_table, lamv, q, k_new, v_new, bias_last, bnew, subln_w.reshape(1, HEAD_W),
      *([cache_k] * pages), *([cache_v] * pages))


def _stack_lane_blocks(x, width=LANES):
    return jnp.concatenate([x[:, c * width:(c + 1) * width] for c in range(x.shape[1] // width)], axis=0)


def _unstack_lane_blocks(y, n):
    return jnp.concatenate([y[c * n:(c + 1) * n] for c in range(y.shape[0] // n)], axis=1)


def _segsum(x, r2_ref):
    n = x.shape[0]
    xs = _stack_lane_blocks(x)
    hi = xs.astype(BF16)
    lo = (xs - hi.astype(F32)).astype(BF16)
    y = jnp.dot(jnp.concatenate([hi, lo], axis=1), r2_ref[...], preferred_element_type=F32)
    return _unstack_lane_blocks(y, n)


def _rwkv_prep_kernel(p_ref, sp_ref, mu_ref, w0_ref, w2_ref, a0_ref, a2_ref, g2_ref, kk_ref, ka_ref, rk_ref, r2_ref,
                      a_out, c_out, w_out, b_out, k_out, v_out, vkr_out, bon_out, g_out, prev_ref, *, tt):
    ti = pl.program_id(1)

    @pl.when(ti == 0)
    def _():
        prev_ref[...] = sp_ref[0]

    p = p_ref[0]
    prev_row = prev_ref[...]
    if tt > 1:
        rolled = pltpu.roll(p, 1, axis=0)
        row = lax.broadcasted_iota(I32, p.shape, 0)
        p_prev = jnp.where(row == 0, prev_row, rolled)
    else:
        p_prev = prev_row
    prev_ref[...] = p[tt - 1:tt, :]
    pm = p + mu_ref[...] * (p_prev - p)
    r = pm[:, 0:RW_W]
    kx = pm[:, RW_W:2 * RW_W]
    vx = pm[:, 2 * RW_W:3 * RW_W]
    lora = pm[:, 3 * RW_W:3 * RW_W + RW_LORA_W]
    gl = pm[:, 3 * RW_W + RW_LORA_W:RW_PROJ]
    dw = jnp.dot(jnp.tanh(lora).astype(BF16), w2_ref[...], preferred_element_type=F32)
    w = -jax.nn.softplus(-(w0_ref[...] + dw)) - 0.5
    decay = jnp.exp(-jnp.exp(w))
    a = jax.nn.sigmoid(a0_ref[...] + jnp.dot(lora.astype(BF16), a2_ref[...], preferred_element_type=F32))
    g = jnp.dot(jax.nn.sigmoid(gl).astype(BF16), g2_ref[...], preferred_element_type=F32)
    kk = kx * kk_ref[...]
    kkn = kk / jnp.maximum(jnp.sqrt(_segsum(kk * kk, r2_ref)), 1e-12)
    k = kx * (1.0 + (a - 1.0) * ka_ref[...])
    av = -kkn
    bv = kkn * a
    br = _segsum(bv * r, r2_ref)
    kr = _segsum(k * r, r2_ref)
    a_out[0] = av
    c_out[0] = decay * r + av * br
    w_out[0] = decay
    b_out[0] = bv
    k_out[0] = k
    v_out[0] = vx
    vkr_out[0] = vx * kr
    bon_out[0] = _segsum(r * k * rk_ref[...], r2_ref) * vx
    g_out[0] = g


def _value_diag():
    row = lax.broadcasted_iota(I32, (RW_N, RW_W), 0)
    lane = lax.broadcasted_iota(I32, (RW_N, RW_W), 1)
    return (lane & (RW_N - 1)) == row


def _scan_step(seq_refs, s_ref, t, e_j, zacc, rmat, diag, *, nb, v_terms):
    a_ref, c_ref, w_ref, b_ref, k_ref, v_ref = seq_refs
    width = rmat.shape[0]
    per = (RW_W // width) * RW_N
    row = lambda ref, bb: ref[bb, pl.ds(t, 1), :]
    nterm = 1 + v_terms
    stacked, pz = [], []
    for bb in range(nb):
        s = s_ref[bb]
        v = row(v_ref, bb)
        v_hi = jnp.where(diag, v, 0.0).astype(BF16)
        terms = [(s * row(a_ref, bb)).astype(BF16), v_hi]
        pz.append((s * row(c_ref, bb)).astype(BF16))
        if v_terms == 2:
            terms.append(jnp.where(diag, v - v_hi.astype(F32), 0.0).astype(BF16))
        stacked += [_stack_lane_blocks(x, width) for x in terms]
    red = jnp.dot(jnp.concatenate(stacked, axis=0), rmat, preferred_element_type=F32)
    for bb in range(nb):
        parts = [_unstack_lane_blocks(red[(nterm * bb + i) * per:(nterm * bb + i + 1) * per], RW_N)
                 for i in range(nterm)]
        vb = parts[1] if v_terms == 1 else parts[1] + parts[2]
        s_ref[bb] = s_ref[bb] * row(w_ref, bb) + parts[0] * row(b_ref, bb) + vb * row(k_ref, bb)
    return zacc + jnp.dot(jnp.concatenate(pz, axis=0), e_j, preferred_element_type=F32)


def _rwkv_scan_kernel(*refs, nb, tt, nst, v_terms):
    seq_refs = refs[:6]
    s0_ref, r_ref, e_ref, y_ref, so_ref, s_ref = refs[6:]
    ti = pl.program_id(1)

    @pl.when(ti == 0)
    def _():
        s_ref[...] = s0_ref[...]

    diag = _value_diag()
    rmat = r_ref[...]

    def group(g, carry):
        zacc = jnp.zeros((nb * RW_N, LANES), F32)
        for j in range(nst):
            zacc = _scan_step(seq_refs, s_ref, g * nst + j, e_ref[j], zacc, rmat, diag, nb=nb, v_terms=v_terms)
        for bb in range(nb):
            y_ref[bb, g] = zacc[bb * RW_N:(bb + 1) * RW_N]
        return carry

    lax.fori_loop(0, tt // nst, group, 0)

    @pl.when(ti == pl.num_programs(1) - 1)
    def _():
        so_ref[...] = s_ref[...]


def _rwkv_post_kernel(y_ref, vkr_ref, bon_ref, g_ref, lw_ref, lb_ref, r2_ref, o_ref):
    y = y_ref[...] + vkr_ref[...]
    mu = _segsum(y, r2_ref) * (1.0 / RW_N)
    d = y - mu
    var = _segsum(d * d, r2_ref) * (1.0 / RW_N)
    yn = d * lax.rsqrt(var + RW_LN_EPS) * lw_ref[...] + lb_ref[...]
    o_ref[...] = ((yn + bon_ref[...]) * g_ref[...]).astype(BF16)


def _segment_ones(width):
    return jnp.kron(jnp.eye(width // RW_N, dtype=F32), jnp.ones((RW_N, RW_N), F32)).astype(BF16)


def _segsum_operand():
    seg = _segment_ones(LANES)
    return jnp.concatenate([seg, seg], axis=0)


def rwkv_prep(prw, shift_prev, prm, batch, seq):
    tt = min(seq, 128)
    r2 = _segsum_operand()
    zeros64 = jnp.zeros((RW_N, RW_W), F32)
    w2p = jnp.concatenate([prm["rw_w2"], zeros64], axis=0).astype(BF16)
    a2p = jnp.concatenate([zeros64, prm["rw_a2"]], axis=0).astype(BF16)
    row = lambda a: a.reshape(1, -1)
    blk_in = pl.BlockSpec((1, tt, RW_PROJ), lambda b, i: (b, i, 0))
    blk_out = pl.BlockSpec((1, tt, RW_W), lambda b, i: (b, i, 0))
    c2 = lambda shape: pl.BlockSpec(shape, lambda b, i: (0,) * len(shape))
    return pl.pallas_call(
        functools.partial(_rwkv_prep_kernel, tt=tt),
        grid=(batch, seq // tt),
        in_specs=[blk_in, pl.BlockSpec((1, 1, RW_PROJ), lambda b, i: (b, 0, 0)), c2((1, RW_PROJ)),
                  c2((1, RW_W)), c2((RW_LORA_W, RW_W)), c2((1, RW_W)), c2((RW_LORA_W, RW_W)),
                  c2((RW_GATE_W, RW_W)), c2((1, RW_W)), c2((1, RW_W)), c2((1, RW_W)), c2((2 * LANES, LANES))],
        out_specs=[blk_out] * 9,
        out_shape=[jax.ShapeDtypeStruct((batch, seq, RW_W), F32)] * 9,
        scratch_shapes=[pltpu.VMEM((1, RW_PROJ), F32)],
        compiler_params=_cparams(("parallel", "arbitrary"), 56),
        name="rwkv_prep",
    )(prw.reshape(batch, seq, RW_PROJ), shift_prev.reshape(batch, 1, RW_PROJ), row(prm["rw_mu"]),
      row(prm["rw_w0"]), w2p, row(prm["rw_a0"]), a2p, prm["rw_g2"].astype(BF16), row(prm["rw_k_k"]),
      row(prm["rw_k_a"]), row(prm["rw_r_k"]), r2)


SCAN_NB = 4


def _scan_group(seq):
    return min(seq, LANES // RW_HEADS)


def _scan_operands(wkv0, batch, seq):
    nst = _scan_group(seq)
    head = jnp.arange(RW_W, dtype=I32) // RW_N
    ecomp = (head[None, :, None] + RW_HEADS * jnp.arange(nst, dtype=I32)[:, None, None]
             == jnp.arange(LANES, dtype=I32)[None, None, :]).astype(BF16)
    s0 = wkv0.astype(F32).transpose(0, 2, 1, 3).reshape(batch, RW_N, RW_W)
    return s0, _segment_ones(MXU_DIM), ecomp


def _scan_out_shapes(batch, seq):
    return [jax.ShapeDtypeStruct((batch, seq // _scan_group(seq), RW_N, LANES), F32),
            jax.ShapeDtypeStruct((batch, RW_N, RW_W), F32)]


def rwkv_scan(seq_arrays, wkv0, batch, seq):
    nst = _scan_group(seq)
    ts = min(seq, 64)
    nb = SCAN_NB
    v_terms = 2 if seq == 1 else 1
    c2 = lambda shape: pl.BlockSpec(shape, lambda b, i: (0,) * len(shape))
    seq_blk = pl.BlockSpec((nb, ts, RW_W), lambda b, i: (b, i, 0))
    grp_blk = pl.BlockSpec((nb, ts // nst, RW_N, LANES), lambda b, i: (b, i, 0, 0))
    st_blk = pl.BlockSpec((nb, RW_N, RW_W), lambda b, i: (b, 0, 0))
    return pl.pallas_call(
        functools.partial(_rwkv_scan_kernel, nb=nb, tt=ts, nst=nst, v_terms=v_terms),
        grid=(batch // nb, seq // ts),
        in_specs=[seq_blk] * 6 + [st_blk, c2((MXU_DIM, MXU_DIM)), c2((nst, RW_W, LANES))],
        out_specs=[grp_blk, st_blk],
        out_shape=_scan_out_shapes(batch, seq),
        scratch_shapes=[pltpu.VMEM((nb, RW_N, RW_W), F32)],
        compiler_params=_cparams(("parallel", "arbitrary"), 48),
        name="rwkv_scan",
    )(*seq_arrays, *_scan_operands(wkv0, batch, seq))


def rwkv_finish(yc, s_fin, vkr, bon, gv, prm, batch, seq):
    m = batch * seq
    nst = _scan_group(seq)
    row = lambda a: a.reshape(1, -1)
    yz = yc.reshape(batch, seq // nst, RW_N, LANES // RW_HEADS, RW_HEADS)[:, :, :, :nst, :]
    yz = yz.transpose(0, 1, 3, 4, 2).reshape(m, RW_W)
    wkv_new = s_fin.reshape(batch, RW_N, RW_HEADS, RW_N).transpose(0, 2, 1, 3)

    tm = min(m, ROW_TILE)
    rows = pl.BlockSpec((tm, RW_W), lambda i: (i, 0))
    y = pl.pallas_call(
        _rwkv_post_kernel,
        grid=(m // tm,),
        in_specs=[rows] * 4 + [_full((1, RW_W)), _full((1, RW_W)), _full((2 * LANES, LANES))],
        out_specs=rows,
        out_shape=jax.ShapeDtypeStruct((m, RW_W), BF16),
        compiler_params=_cparams(("parallel",), 48),
        name="rwkv_post",
    )(yz, vkr.reshape(m, RW_W), bon.reshape(m, RW_W), gv.reshape(m, RW_W), row(prm["rw_ln_w"]),
      row(prm["rw_ln_b"]), _segsum_operand())
    return y, wkv_new


def rwkv_mix(prw, shift_prev, wkv0, prm, batch, seq):
    *seq_arrays, vkr, bon, gv = rwkv_prep(prw, shift_prev, prm, batch, seq)
    yc, s_fin = rwkv_scan(seq_arrays[:6], wkv0, batch, seq)
    return rwkv_finish(yc, s_fin, vkr, bon, gv, prm, batch, seq)


def _merge_kernel(ya_ref, yb_ref, ga_ref, gb_ref, ba_ref, bb_ref, wa_ref, wb_ref, o_ref):
    pa = jnp.dot(ya_ref[...].astype(BF16), wa_ref[...], preferred_element_type=F32)
    pb = jnp.dot(yb_ref[...], wb_ref[...], preferred_element_type=F32)
    ga = jax.nn.sigmoid(ga_ref[...] + ba_ref[...])
    gb = jax.nn.sigmoid(gb_ref[...] + bb_ref[...])
    o_ref[...] = (ga * pa + gb * pb).astype(BF16)


def merge_branches(y_att, y_rw, g_lin, b_gate, wa, wb):
    m = y_att.shape[0]
    tm = min(m, ROW_TILE)
    d = D_MODEL
    return pl.pallas_call(
        _merge_kernel,
        grid=(m // tm,),
        in_specs=[pl.BlockSpec((tm, ATT_W), lambda i: (i, 0)), pl.BlockSpec((tm, RW_W), lambda i: (i, 0)),
                  pl.BlockSpec((tm, d), lambda i: (i, 0)), pl.BlockSpec((tm, d), lambda i: (i, 1)),
                  pl.BlockSpec((1, d), lambda i: (0, 0)), pl.BlockSpec((1, d), lambda i: (0, 1)),
                  _full((ATT_W, d)), _full((RW_W, d))],
        out_specs=pl.BlockSpec((tm, d), lambda i: (i, 0)),
        out_shape=jax.ShapeDtypeStruct((m, d), BF16),
        compiler_params=_cparams(("parallel",), 48),
        name="merge_branches",
    )(y_att, y_rw, g_lin, g_lin, b_gate.reshape(1, 2 * d), b_gate.reshape(1, 2 * d), wa, wb)


SLAB = D_MODEL // LANES


def _store_token_slabs(ref, x):
    n = x.shape[0]
    for c in range(SLAB):
        ref[pl.ds(c, n, stride=SLAB), :] = x[:, c * LANES:(c + 1) * LANES]


def _load_token_slabs(ref, n):
    return jnp.concatenate([ref[pl.ds(c, n, stride=SLAB), :] for c in range(SLAB)], axis=1)


def _outproj_router_kernel(mg_ref, wo_ref, x_ref, nw_ref, wr_ref, br_ref, h_ref, hn_ref, eid_ref, ew_ref):
    h = x_ref[...] + jnp.dot(mg_ref[...], wo_ref[...], preferred_element_type=F32)
    h_ref[...] = h
    ms = jnp.mean(h * h, axis=-1, keepdims=True)
    hn = h * lax.rsqrt(ms + NORM_EPS) * nw_ref[...]
    _store_token_slabs(hn_ref, hn)
    lg = jnp.dot(hn.astype(BF16), wr_ref[...], preferred_element_type=F32) + br_ref[...]
    lane = lax.broadcasted_iota(I32, lg.shape, 1)
    lane_f = lane.astype(F32)
    big = float(LANES)
    lgm = jnp.where(lane < N_GROUPS, lg, NEG)
    gmax = jnp.max(lgm, axis=-1, keepdims=True)
    gidx = jnp.min(jnp.where(lgm == gmax, lane_f, big), axis=-1, keepdims=True)
    g_w = 1.0 / jnp.sum(jnp.exp(lgm - gmax), axis=-1, keepdims=True)
    lane_group = ((lane >> 3) - 1).astype(F32)
    in_group = (lane >= N_GROUPS) & (lane < N_GROUPS + N_EXPERTS) & (lane_group == gidx)
    le = jnp.where(in_group, lg, NEG)
    m1 = jnp.max(le, axis=-1, keepdims=True)
    i1 = jnp.min(jnp.where(le == m1, lane_f, big), axis=-1, keepdims=True)
    le2 = jnp.where(lane_f == i1, NEG, le)
    m2 = jnp.max(le2, axis=-1, keepdims=True)
    i2 = jnp.min(jnp.where(le2 == m2, lane_f, big), axis=-1, keepdims=True)
    e2 = jnp.exp(m2 - m1)
    w1 = g_w / (1.0 + e2)
    w2 = g_w * e2 / (1.0 + e2)
    col = lax.broadcasted_iota(I32, eid_ref.shape, 1)
    eid_ref[...] = jnp.where(col == 0, i1, i2).astype(I32) - N_GROUPS
    ew_ref[...] = jnp.where(col == 0, w1, w2)


def outproj_router(merged, w_out, x, norm_w, w_router, b_router):
    m, d = x.shape
    tm = min(m, ROW_TILE)
    rows = lambda w: pl.BlockSpec((tm, w), lambda i: (i, 0))
    return pl.pallas_call(
        _outproj_router_kernel,
        grid=(m // tm,),
        in_specs=[rows(d), _full((d, d)), rows(d), _full((1, d)), _full((d, LANES)), _full((1, LANES))],
        out_specs=[rows(d), pl.BlockSpec((tm * SLAB, LANES), lambda i: (i, 0)), rows(2), rows(2)],
        out_shape=[jax.ShapeDtypeStruct((m, d), F32), jax.ShapeDtypeStruct((m * SLAB, LANES), F32),
                   jax.ShapeDtypeStruct((m, 2), I32), jax.ShapeDtypeStruct((m, 2), F32)],
        compiler_params=_cparams(("parallel",), 48),
        name="outproj_router",
    )(merged, w_out, x, norm_w.reshape(1, d), w_router, b_router)


def _moe_kernel(starts_ref, counts_ref, pcounts_ref, src_ref, hp_ref, hs_ref, wg_ref, wu_ref, wd_ref,
                y_ref, wgb, wub, wdb, xbuf, ybuf, sem_in, sem_out, *, n_prompt):
    e = pl.program_id(0)

    @pl.when(e == 0)
    def _():
        xbuf[...] = jnp.zeros(xbuf.shape, F32)

    n = counts_ref[e]
    n_p = pcounts_ref[e]
    start = starts_ref[e]
    nblk = (n + MOE_BLK - 1) // MOE_BLK

    def slab(i):
        return pl.ds(pl.multiple_of(i * SLAB, SLAB), SLAB)

    def row_in(src_hbm, tok, slot, r):
        return pltpu.make_async_copy(src_hbm.at[slab(tok)], xbuf.at[slot, slab(r)], sem_in.at[slot])

    def row_out(dst, r):
        return pltpu.make_async_copy(ybuf.at[slab(r)], y_ref.at[slab(dst)], sem_out)

    def for_rows(lo, hi, body):
        whole = (lo == 0) & (hi == MOE_BLK)

        @pl.when(whole)
        def _():
            def chunk(i, c):
                for u in range(MOE_UNROLL):
                    body(i * MOE_UNROLL + u)
                return c
            lax.fori_loop(0, MOE_BLK // MOE_UNROLL, chunk, 0)

        @pl.when(jnp.logical_not(whole))
        def _():
            lax.fori_loop(lo, hi, lambda r, c: (body(r), c)[1], 0)

    def block_rows(bi):
        nv = jnp.minimum(MOE_BLK, n - bi * MOE_BLK)
        nvp = jnp.clip(n_p - bi * MOE_BLK, 0, nv)
        return start + bi * MOE_BLK, nv, nvp

    def gather_start(bi, slot):
        r0, nv, nvp = block_rows(bi)
        for_rows(0, nvp, lambda r: row_in(hp_ref, src_ref[r0 + r] >> 1, slot, r).start())
        for_rows(nvp, nv, lambda r: row_in(hs_ref, (src_ref[r0 + r] >> 1) - n_prompt, slot, r).start())

    def gather_wait(bi, slot):
        _, nv, _ = block_rows(bi)
        for_rows(0, nv, lambda r: row_in(hp_ref, 0, slot, r).wait())

    def block_out(r0):
        dst = pl.ds(pl.multiple_of(r0 * SLAB, SLAB), MOE_BLK * SLAB)
        return pltpu.make_async_copy(ybuf, y_ref.at[dst], sem_out)

    def writeback(bi, wait):
        r0, nv, _ = block_rows(bi)

        @pl.when(nv == MOE_BLK)
        def _():
            if wait:
                block_out(0).wait()
            else:
                block_out(r0).start()

        @pl.when(nv < MOE_BLK)
        def _():
            def one(r, c):
                if wait:
                    row_out(0, r).wait()
                else:
                    row_out(r0 + r, r).start()
                return c
            lax.fori_loop(0, nv, one, 0)

    scatter_start = functools.partial(writeback, wait=False)
    scatter_wait = functools.partial(writeback, wait=True)

    @pl.when(n > 0)
    def _():
        gather_start(0, 0)
        wgb[...] = wg_ref[0].astype(BF16)
        wub[...] = wu_ref[0].astype(BF16)
        wdb[...] = wd_ref[0].astype(BF16)

        def block(bi, carry):
            slot = bi & 1
            gather_wait(bi, slot)

            @pl.when(bi + 1 < nblk)
            def _():
                gather_start(bi + 1, 1 - slot)

            x = _load_token_slabs(xbuf.at[slot], MOE_BLK).astype(BF16)
            hg = jnp.dot(x, wgb[...], preferred_element_type=F32)
            hu = jnp.dot(x, wub[...], preferred_element_type=F32)
            hd = (hg * jax.nn.sigmoid(hg) * hu).astype(BF16)
            y = jnp.dot(hd, wdb[...], preferred_element_type=F32)

            @pl.when(bi > 0)
            def _():
                scatter_wait(bi - 1)

            _store_token_slabs(ybuf, y)
            scatter_start(bi)
            return carry

        lax.fori_loop(0, nblk, block, 0)
        scatter_wait(nblk - 1)


def moe_experts(hn_p, hn_s, eid, w_gate, w_up, w_down):
    d = D_MODEL
    n_prompt = hn_p.shape[0] // SLAB
    n_rows = eid.shape[0] * 2
    flat_e = eid.reshape(n_rows)
    onehot = (flat_e[:, None] == jnp.arange(N_EXPERTS, dtype=I32)[None, :]).astype(I32)
    counts = jnp.sum(onehot, axis=0)
    pcounts = jnp.sum(onehot[:2 * n_prompt], axis=0)
    starts = jnp.cumsum(counts) - counts
    rank = jnp.sum(onehot * (jnp.cumsum(onehot, axis=0) - 1), axis=1)
    pos = starts[flat_e] + rank
    src = jnp.zeros((n_rows,), I32).at[pos].set(jnp.arange(n_rows, dtype=I32), unique_indices=True,
                                                mode="promise_in_bounds")
    any_spec = pl.BlockSpec(memory_space=pl.ANY)
    grid_spec = pltpu.PrefetchScalarGridSpec(
        num_scalar_prefetch=4,
        grid=(N_EXPERTS,),
        in_specs=[any_spec, any_spec,
                  pl.BlockSpec((1, d, D_FF), lambda e, *_: (e, 0, 0)),
                  pl.BlockSpec((1, d, D_FF), lambda e, *_: (e, 0, 0)),
                  pl.BlockSpec((1, D_FF, d), lambda e, *_: (e, 0, 0))],
        out_specs=any_spec,
        scratch_shapes=[pltpu.VMEM((d, D_FF), BF16), pltpu.VMEM((d, D_FF), BF16), pltpu.VMEM((D_FF, d), BF16),
                        pltpu.VMEM((2, MOE_BLK * SLAB, LANES), F32), pltpu.VMEM((MOE_BLK * SLAB, LANES), F32),
                        pltpu.SemaphoreType.DMA((2,)), pltpu.SemaphoreType.DMA(())],
    )
    y_sorted = pl.pallas_call(
        functools.partial(_moe_kernel, n_prompt=n_prompt),
        grid_spec=grid_spec,
        out_shape=jax.ShapeDtypeStruct((n_rows * SLAB, LANES), F32),
        compiler_params=_cparams(("arbitrary",), 56),
        name="moe_experts",
    )(starts, counts, pcounts, src, hn_p, hn_s, w_gate, w_up, w_down)
    return y_sorted, pos.reshape(-1, 2)


def _final_kernel(pos_ref, h_ref, ew_ref, nw_ref, y_ref, o_ref, ybuf, sem, *, tok0):
    tm = h_ref.shape[0]
    base = (tok0 + pl.program_id(0) * tm) * 2

    def slab(i):
        return pl.ds(pl.multiple_of(i * SLAB, SLAB), SLAB)

    def fetch(r):
        return pltpu.make_async_copy(y_ref.at[slab(pos_ref[base + r])], ybuf.at[slab((r & 1) * tm + (r >> 1))], sem)

    def chunk(start):
        def body(i, c):
            for u in range(MOE_UNROLL):
                r = i * MOE_UNROLL + u
                fetch(r).start() if start else fetch(r).wait()
            return c
        lax.fori_loop(0, 2 * tm // MOE_UNROLL, body, 0)

    chunk(True)
    chunk(False)
    ew = ew_ref[...]
    x = (h_ref[...] + ew[:, 0:1] * _load_token_slabs(ybuf.at[pl.ds(0, tm * SLAB)], tm)
         + ew[:, 1:2] * _load_token_slabs(ybuf.at[pl.ds(tm * SLAB, tm * SLAB)], tm))
    ms = jnp.mean(x * x, axis=-1, keepdims=True)
    o_ref[...] = x * lax.rsqrt(ms + NORM_EPS) * nw_ref[...]


def combine_final(h, y_sorted, pos, ew, norm_w, tok0):
    m, d = h.shape
    tm = min(m, ROW_TILE)
    assert (2 * tm) % MOE_UNROLL == 0
    rows = lambda w: pl.BlockSpec((tm, w), lambda i, pos_: (i, 0))
    grid_spec = pltpu.PrefetchScalarGridSpec(
        num_scalar_prefetch=1,
        grid=(m // tm,),
        in_specs=[rows(d), rows(2), pl.BlockSpec((1, d), lambda i, pos_: (0, 0)), pl.BlockSpec(memory_space=pl.ANY)],
        out_specs=rows(d),
        scratch_shapes=[pltpu.VMEM((2 * tm * SLAB, LANES), F32), pltpu.SemaphoreType.DMA(())],
    )
    return pl.pallas_call(
        functools.partial(_final_kernel, tok0=tok0),
        grid_spec=grid_spec,
        out_shape=jax.ShapeDtypeStruct((m, d), F32),
        compiler_params=_cparams(("arbitrary",), 48),
        name="combine_final",
    )(pos.reshape(-1), h, ew, norm_w.reshape(1, d), y_sorted)


_RW_COL0 = 3 * ATT_W
_GATE_COL0 = _RW_COL0 + RW_PROJ


def _in_proj(x2, prm, q_dtype):
    xn = rmsnorm_bf16(x2, prm["norm_mix_w"])
    w_in = prm["w_in"]
    q = matmul_cols(xn, w_in, 0, ATT_W, q_dtype)
    k = matmul_cols(xn, w_in, ATT_W, ATT_W, F32)
    v = matmul_cols(xn, w_in, 2 * ATT_W, ATT_W, F32)
    prw = matmul_cols(xn, w_in, _RW_COL0, RW_PROJ, F32)
    g_lin = matmul_cols(xn, w_in, _GATE_COL0, 2 * D_MODEL, F32)
    return q, k, v, prw, g_lin


def _mixer_tail(x2, y_att, y_rw, g_lin, prm):
    merged = merge_branches(y_att, y_rw, g_lin, prm["b_gate"], prm["wa_bf16"], prm["wb_bf16"])
    return outproj_router(merged, prm["wo_bf16"], x2, prm["norm_ffn_w"], prm["w_router"], prm["b_router"])


def kernel(x_prompt, x_sample, cache_k, cache_v, state_wkv, state_shift, page_table, norm_mix_w, w_in, b_gate, lambda_q1, lambda_k1, lambda_q2, lambda_k2, subln_w, rel_bias, rw_mu, rw_w0, rw_w2, rw_a0, rw_a2, rw_g2, rw_k_k, rw_k_a, rw_r_k, rw_ln_w, rw_ln_b, w_branch_a, w_branch_b, w_out, norm_ffn_w, w_router_g, b_router_g, w_router_e, b_router_e, w_gate_e, w_up_e, w_down_e, norm_final_w):
    assert w_in.shape[0] == 1, "single-layer trunk"
    bp, sp, d = x_prompt.shape
    bs, ss, _ = x_sample.shape
    assert ss == 1 and d == D_MODEL
    n_pool = cache_k.shape[1]
    pad_lanes = LANES - N_GROUPS - N_EXPERTS
    prm = {
        "norm_mix_w": norm_mix_w[0], "w_in": w_in[0], "b_gate": b_gate[0],
        "rw_mu": rw_mu[0], "rw_w0": rw_w0[0], "rw_w2": rw_w2[0], "rw_a0": rw_a0[0], "rw_a2": rw_a2[0],
        "rw_g2": rw_g2[0], "rw_k_k": rw_k_k[0], "rw_k_a": rw_k_a[0], "rw_r_k": rw_r_k[0].reshape(RW_W),
        "rw_ln_w": rw_ln_w[0], "rw_ln_b": rw_ln_b[0],
        "wa_bf16": w_branch_a[0].astype(BF16), "wb_bf16": w_branch_b[0].astype(BF16),
        "wo_bf16": w_out[0].astype(BF16), "norm_ffn_w": norm_ffn_w[0],
        "w_router": jnp.concatenate([w_router_g[0], w_router_e[0], jnp.zeros((d, pad_lanes), F32)],
                                    axis=1).astype(BF16),
        "b_router": jnp.concatenate([b_router_g[0], b_router_e[0], jnp.zeros((pad_lanes,), F32)]).reshape(1, LANES),
    }
    lamv = jnp.stack([lambda_q1[0], lambda_k1[0], lambda_q2[0], lambda_k2[0]]).astype(F32)
    sub_w = subln_w[0]

    mp = bp * sp
    xp2 = x_prompt.reshape(mp, d)
    q_p, k_p, v_p, prw_p, g_p = _in_proj(xp2, prm, BF16)
    att_p = attn_prompt(q_p, k_p, v_p, lamv, rel_bias, sub_w, bp, sp)
    rw_p, wkv_p = rwkv_mix(prw_p, jnp.zeros((bp, RW_PROJ), F32), jnp.zeros((bp, RW_HEADS, RW_N, RW_N), F32),
                           prm, bp, sp)
    h_p, hn_p, eid_p, ew_p = _mixer_tail(xp2, att_p, rw_p, g_p, prm)

    xs2 = x_sample.reshape(bs, d)
    q_s, k_s, v_s, prw_s, g_s = _in_proj(xs2, prm, F32)
    heads = lambda a: a.reshape(bs, ATT_HEADS, HEAD_W)
    att_s = attn_decode(heads(q_s), heads(k_s), heads(v_s),
                        cache_k[0].reshape(n_pool, PAGE * ATT_HEADS, HEAD_W),
                        cache_v[0].reshape(n_pool, PAGE * ATT_HEADS, HEAD_W),
                        page_table, lamv, rel_bias, sub_w).reshape(bs, ATT_W)
    rw_s, wkv_s = rwkv_mix(prw_s, state_shift[0], state_wkv[0], prm, bs, 1)
    h_s, hn_s, eid_s, ew_s = _mixer_tail(xs2, att_s, rw_s, g_s, prm)

    y_sorted, pos = moe_experts(hn_p, hn_s, jnp.concatenate([eid_p, eid_s], axis=0),
                                w_gate_e[0], w_up_e[0], w_down_e[0])
    y_p = combine_final(h_p, y_sorted, pos, ew_p, norm_final_w, 0)
    y_s = combine_final(h_s, y_sorted, pos, ew_s, norm_final_w, mp)

    return (y_p.reshape(bp, sp, d), y_s.reshape(bs, 1, d),
            k_p.reshape(1, bp, sp, ATT_HEADS, HEAD_W), v_p.reshape(1, bp, sp, ATT_HEADS, HEAD_W),
            wkv_p.astype(state_wkv.dtype)[None], prw_p.reshape(bp, sp, RW_PROJ)[None, :, -1],
            k_s.reshape(1, bs, 1, ATT_HEADS, HEAD_W), v_s.reshape(1, bs, 1, ATT_HEADS, HEAD_W),
            wkv_s.astype(state_wkv.dtype)[None], prw_s.reshape(1, bs, RW_PROJ))
```

```python
import functools
import math

import jax
import jax.numpy as jnp
from jax import lax
from jax.experimental import pallas as pl
from jax.experimental.pallas import tpu as pltpu

F32 = jnp.float32
BF16 = jnp.bfloat16
I32 = jnp.int32

D_MODEL = 2048
ATT_HEADS = 8
ATT_DK = 64
HEAD_W = 2 * ATT_DK
ATT_W = ATT_HEADS * HEAD_W
ATT_SCALE = ATT_DK ** -0.5
RW_HEADS = 16
RW_N = 64
RW_W = RW_HEADS * RW_N
RW_LORA_W = 128
RW_GATE_W = 128
RW_PROJ = 3 * RW_W + RW_LORA_W + RW_GATE_W
N_BUCKETS = 32
MAX_DISTANCE = 128
PAGE = 128
EXPERTS_PER_GROUP = 8
N_GROUPS = 8
N_EXPERTS = N_GROUPS * EXPERTS_PER_GROUP
D_FF = 512
NORM_EPS = 1e-6
SUBLN_EPS = 1e-5
RW_LN_EPS = 64e-5
LAM_INIT = 0.8 - 0.6 * math.exp(-0.3 * 0)
NEG = -1e30
LANES = 128
MXU_DIM = 256
MOE_BLK = 384
MOE_UNROLL = 16
ROW_TILE = 256
MIB = 1024 * 1024


def _cparams(sem, vmem_mib):
    return pltpu.CompilerParams(dimension_semantics=sem, vmem_limit_bytes=vmem_mib * MIB)


def _full(shape):
    nd = len(shape)
    return pl.BlockSpec(shape, lambda *_: (0,) * nd)


def _rmsnorm_bf16_kernel(x_ref, w_ref, o_ref):
    x = x_ref[...]
    ms = jnp.mean(x * x, axis=-1, keepdims=True)
    o_ref[...] = (x * lax.rsqrt(ms + NORM_EPS) * w_ref[...]).astype(BF16)


def rmsnorm_bf16(x, w):
    m, d = x.shape
    tm = min(m, 512)
    return pl.pallas_call(
        _rmsnorm_bf16_kernel,
        grid=(m // tm,),
        in_specs=[pl.BlockSpec((tm, d), lambda i: (i, 0)), _full((1, d))],
        out_specs=pl.BlockSpec((tm, d), lambda i: (i, 0)),
        out_shape=jax.ShapeDtypeStruct((m, d), BF16),
        compiler_params=_cparams(("parallel",), 32),
        name="rmsnorm_bf16",
    )(x, w.reshape(1, d))


def _matmul_kernel(x_ref, w_ref, o_ref):
    o_ref[...] = jnp.dot(x_ref[...], w_ref[...].astype(BF16),
                         preferred_element_type=F32).astype(o_ref.dtype)


def matmul_cols(x, w, col0, ncols, out_dtype):
    m, k = x.shape
    tm = next(t for t in (2048, 1024, 512, 256, m) if m % t == 0)
    tn = next(t for t in (512, 256, 128) if col0 % t == 0 and ncols % t == 0)
    off = col0 // tn
    return pl.pallas_call(
        _matmul_kernel,
        grid=(m // tm, ncols // tn),
        in_specs=[pl.BlockSpec((tm, k), lambda i, j: (i, 0)),
                  pl.BlockSpec((k, tn), lambda i, j: (0, j + off))],
        out_specs=pl.BlockSpec((tm, tn), lambda i, j: (i, j)),
        out_shape=jax.ShapeDtypeStruct((m, ncols), out_dtype),
        compiler_params=_cparams(("parallel", "arbitrary"), 48),
        name="in_proj",
    )(x, w)


def _rel_bucket(dist):
    n = jnp.maximum(dist, 0)
    max_exact = N_BUCKETS // 2
    large = max_exact + (jnp.log(jnp.maximum(n, 1).astype(F32) / max_exact)
                         / math.log(MAX_DISTANCE / max_exact) * (N_BUCKETS - max_exact)).astype(I32)
    large = jnp.minimum(large, N_BUCKETS - 1)
    return jnp.where(n < max_exact, n, large)


def _bias_by_distance(rel_bias, n):
    d = jnp.arange(n, dtype=I32)
    table = rel_bias.astype(F32)[_rel_bucket(d)]
    far = rel_bias.astype(F32)[N_BUCKETS - 1]
    return (table - far[None, :]).T


def _toeplitz(vals, t):
    h = vals.shape[0]
    w = jnp.concatenate([vals[:, ::-1], jnp.zeros((h, 1), vals.dtype)], axis=1)
    flat = jnp.tile(w, (1, t))[:, :t * (2 * t - 1)]
    return flat.reshape(h, t, 2 * t - 1)[:, :, t - 1:]


def _lambda(lamv_ref):
    lv = lamv_ref[...]
    s1 = jnp.sum(lv[0:1] * lv[1:2], axis=-1, keepdims=True)
    s2 = jnp.sum(lv[2:3] * lv[3:4], axis=-1, keepdims=True)
    return jnp.exp(s1) - jnp.exp(s2) + LAM_INIT


def _split_halves(q):
    lane = lax.broadcasted_iota(I32, q.shape, 1)
    return jnp.concatenate([jnp.where(lane < ATT_DK, q, 0.0), jnp.where(lane >= ATT_DK, q, 0.0)], axis=0)


def _attn_prompt_kernel(lamv_ref, q_ref, k_ref, v_ref, bd_ref, bp_ref, sw_ref, o_ref, kb_ref, v1_ref, *, t, nq):
    kb_ref[...] = k_ref[...].astype(BF16)
    v1_ref[:, :HEAD_W] = v_ref[...].astype(BF16)
    v1_ref[:, HEAD_W:] = jnp.ones((v1_ref.shape[0], HEAD_W), BF16)
    lam = _lambda(lamv_ref)
    sw = sw_ref[...] * (1.0 - LAM_INIT)
    bd = bd_ref[0]
    bp = bp_ref[0]
    bd2 = jnp.concatenate([bd, bd], axis=0)
    bp2 = jnp.concatenate([bp, bp], axis=0)

    for i in range(nq):
        n_keys = (i + 1) * t
        q = q_ref[i * t:(i + 1) * t, :].astype(F32) * ATT_SCALE
        q2 = _split_halves(q).astype(BF16)
        s = lax.dot_general(q2, kb_ref[:n_keys, :], (((1,), (1,)), ((), ())), preferred_element_type=F32)
        near = [s[:, i * t:] + bd2]
        if i >= 1:
            near.insert(0, s[:, (i - 1) * t:i * t] + bp2)
        s = jnp.concatenate(([s[:, :(i - 1) * t]] if i >= 2 else []) + near, axis=1)
        e = jnp.exp(s - jnp.max(s, axis=-1, keepdims=True)).astype(BF16)
        a = jnp.dot(e, v1_ref[:n_keys, :], preferred_element_type=F32)
        a = a[:, :HEAD_W] / a[:, HEAD_W:]
        o = a[:t] - lam * a[t:]
        ms = jnp.mean(o * o, axis=-1, keepdims=True)
        o_ref[i * t:(i + 1) * t, :] = o * lax.rsqrt(ms + SUBLN_EPS) * sw


def attn_prompt(q, k, v, lamv, rel_bias, subln_w, batch, seq):
    t = min(seq, 256)
    nq = seq // t
    assert t >= LANES, "blocks two or more away must lie beyond the last distinct distance bucket"
    bd0 = _bias_by_distance(rel_bias, 2 * t)
    bdiag = _toeplitz(jnp.concatenate([jnp.full((ATT_HEADS, t - 1), NEG, F32), bd0[:, :t]], axis=1), t)
    bprev = _toeplitz(bd0[:, 1:], t)
    blk = lambda: pl.BlockSpec((seq, HEAD_W), lambda b, h: (b, h))
    return pl.pallas_call(
        functools.partial(_attn_prompt_kernel, t=t, nq=nq),
        grid=(batch, ATT_HEADS),
        in_specs=[_full((4, ATT_DK)), blk(), blk(), blk(),
                  pl.BlockSpec((1, t, t), lambda b, h: (h, 0, 0)),
                  pl.BlockSpec((1, t, t), lambda b, h: (h, 0, 0)),
                  _full((1, HEAD_W))],
        out_specs=blk(),
        out_shape=jax.ShapeDtypeStruct((batch * seq, ATT_W), F32),
        scratch_shapes=[pltpu.VMEM((seq, HEAD_W), BF16), pltpu.VMEM((seq, 2 * HEAD_W), BF16)],
        compiler_params=_cparams(("parallel", "parallel"), 48),
        name="attn_prompt",
    )(lamv, q, k, v, bdiag, bprev, subln_w.reshape(1, HEAD_W))


def _attn_decode_kernel(pt_ref, lamv_ref, q_ref, kn_ref, vn_ref, bias_ref, bnew_ref, sw_ref, *rest, pages):
    del pt_ref
    k_refs = rest[:pages]
    v_refs = rest[pages:2 * pages]
    o_ref, m_ref, l_ref, acc_ref = rest[2 * pages:]
    c = pl.program_id(1)
    last = pl.num_programs(1) - 1

    @pl.when(c == 0)
    def _():
        m_ref[...] = jnp.full(m_ref.shape, NEG, F32)
        l_ref[...] = jnp.zeros(l_ref.shape, F32)
        acc_ref[...] = jnp.zeros(acc_ref.shape, F32)

    q16 = _split_halves(q_ref[0] * ATT_SCALE)
    row_head = lax.broadcasted_iota(I32, (2 * ATT_HEADS, 1), 0) & (ATT_HEADS - 1)
    qh = [jnp.where(row_head == h, q16, 0.0).astype(BF16) for h in range(ATT_HEADS)]
    pairs = pages // 2
    st = {"s": [None] * pairs, "pv": jnp.zeros((2 * ATT_HEADS, HEAD_W), F32)}

    def head_rows(ref, h):
        return ref[0, pl.ds(h, PAGE, stride=ATT_HEADS), :]

    def two_pages(refs, pp, h):
        return jnp.concatenate([head_rows(refs[2 * pp], h), head_rows(refs[2 * pp + 1], h)], axis=0).astype(BF16)

    def logits(pp):
        sp = None
        for h in range(ATT_HEADS):
            d = lax.dot_general(qh[h], two_pages(k_refs, pp, h), (((1,), (1,)), ((), ())),
                                preferred_element_type=F32)
            sp = d if sp is None else sp + d
        st["s"][pp] = sp

    def softmax_update():
        s = jnp.concatenate(st["s"], axis=1)
        s = s + bias_ref[...] * (c == last).astype(F32)
        m = m_ref[...]
        mn = jnp.maximum(m, jnp.max(s, axis=-1, keepdims=True))
        st["alpha"] = jnp.exp(m - mn)
        e = jnp.exp(s - mn)
        l_ref[...] = st["alpha"] * l_ref[...] + jnp.sum(e, axis=-1, keepdims=True)
        m_ref[...] = mn
        st["e"] = e.astype(BF16)

    def weighted_values(pp):
        ep = st["e"][:, pp * 2 * PAGE:(pp + 1) * 2 * PAGE]
        for h in range(ATT_HEADS):
            r = jnp.dot(ep, two_pages(v_refs, pp, h), preferred_element_type=F32)
            st["pv"] = st["pv"] + jnp.where(row_head == h, r, 0.0)

    def accumulate():
        acc_ref[...] = st["alpha"] * acc_ref[...] + st["pv"]

    for pp in range(pairs):
        logits(pp)
    softmax_update()
    for pp in range(pairs):
        weighted_values(pp)
    accumulate()

    @pl.when(c == last)
    def _():
        lam = _lambda(lamv_ref)
        m0 = m_ref[...]
        k16 = jnp.concatenate([kn_ref[0], kn_ref[0]], axis=0)
        v16 = jnp.concatenate([vn_ref[0], vn_ref[0]], axis=0)
        s_new = jnp.sum(q16 * k16, axis=-1, keepdims=True) + bnew_ref[...]
        m1 = jnp.maximum(m0, s_new)
        a1 = jnp.exp(m0 - m1)
        e_new = jnp.exp(s_new - m1)
        l1 = a1 * l_ref[...] + e_new
        acc1 = a1 * acc_ref[...] + e_new * v16
        o = acc1[:ATT_HEADS] / l1[:ATT_HEADS] - lam * (acc1[ATT_HEADS:] / l1[ATT_HEADS:])
        ms = jnp.mean(o * o, axis=-1, keepdims=True)
        o_ref[0] = o * lax.rsqrt(ms + SUBLN_EPS) * (sw_ref[...] * (1.0 - LAM_INIT))


def attn_decode(q, k_new, v_new, cache_k, cache_v, page_table, lamv, rel_bias, subln_w):
    nb, n_pages = page_table.shape
    pages = next(g for g in (8, 4, 2) if n_pages % g == 0)
    n_chunks = n_pages // pages
    width = pages * PAGE
    bd0 = _bias_by_distance(rel_bias, width + 1)
    near = bd0[:, width - jnp.arange(width, dtype=I32)]
    bias_last = jnp.concatenate([near, near], axis=0)
    bnew = jnp.concatenate([bd0[:, 0:1], bd0[:, 0:1]], axis=0)
    row = lambda: pl.BlockSpec((1, ATT_HEADS, HEAD_W), lambda b, c, pt: (b, 0, 0))
    page = lambda g: pl.BlockSpec((1, PAGE * ATT_HEADS, HEAD_W), lambda b, c, pt: (pt[b, c * pages + g], 0, 0))
    const = lambda shape: pl.BlockSpec(shape, lambda b, c, pt: (0,) * len(shape))
    grid_spec = pltpu.PrefetchScalarGridSpec(
        num_scalar_prefetch=1,
        grid=(nb, n_chunks),
        in_specs=[const((4, ATT_DK)), row(), row(), row(), const((2 * ATT_HEADS, width)),
                  const((2 * ATT_HEADS, 1)), const((1, HEAD_W))]
                 + [page(g) for g in range(pages)] + [page(g) for g in range(pages)],
        out_specs=row(),
        scratch_shapes=[pltpu.VMEM((2 * ATT_HEADS, 1), F32), pltpu.VMEM((2 * ATT_HEADS, 1), F32),
                        pltpu.VMEM((2 * ATT_HEADS, HEAD_W), F32)],
    )
    return pl.pallas_call(
        functools.partial(_attn_decode_kernel, pages=pages),
        grid_spec=grid_spec,
        out_shape=jax.ShapeDtypeStruct((nb, ATT_HEADS, HEAD_W), F32),
        compiler_params=_cparams(("parallel", "arbitrary"), 56),
        name="attn_decode",
    )(page_table, lamv, q, k_new, v_new, bias_last, bnew, subln_w.reshape(1, HEAD_W),
      *([cache_k] * pages), *([cache_v] * pages))


def _stack_lane_blocks(x, width=LANES):
    return jnp.concatenate([x[:, c * width:(c + 1) * width] for c in range(x.shape[1] // width)], axis=0)


def _unstack_lane_blocks(y, n):
    return jnp.concatenate([y[c * n:(c + 1) * n] for c in range(y.shape[0] // n)], axis=1)


def _segsum(x, r2_ref):
    n = x.shape[0]
    xs = _stack_lane_blocks(x)
    hi = xs.astype(BF16)
    lo = (xs - hi.astype(F32)).astype(BF16)
    y = jnp.dot(jnp.concatenate([hi, lo], axis=1), r2_ref[...], preferred_element_type=F32)
    return _unstack_lane_blocks(y, n)


def _rwkv_prep_kernel(p_ref, sp_ref, mu_ref, w0_ref, w2_ref, a0_ref, a2_ref, g2_ref, kk_ref, ka_ref, rk_ref, r2_ref,
                      a_out, c_out, w_out, b_out, k_out, v_out, vkr_out, bon_out, g_out, prev_ref, *, tt):
    ti = pl.program_id(1)

    @pl.when(ti == 0)
    def _():
        prev_ref[...] = sp_ref[0]

    p = p_ref[0]
    prev_row = prev_ref[...]
    if tt > 1:
        rolled = pltpu.roll(p, 1, axis=0)
        row = lax.broadcasted_iota(I32, p.shape, 0)
        p_prev = jnp.where(row == 0, prev_row, rolled)
    else:
        p_prev = prev_row
    prev_ref[...] = p[tt - 1:tt, :]
    pm = p + mu_ref[...] * (p_prev - p)
    r = pm[:, 0:RW_W]
    kx = pm[:, RW_W:2 * RW_W]
    vx = pm[:, 2 * RW_W:3 * RW_W]
    lora = pm[:, 3 * RW_W:3 * RW_W + RW_LORA_W]
    gl = pm[:, 3 * RW_W + RW_LORA_W:RW_PROJ]
    dw = jnp.dot(jnp.tanh(lora).astype(BF16), w2_ref[...], preferred_element_type=F32)
    w = -jax.nn.softplus(-(w0_ref[...] + dw)) - 0.5
    decay = jnp.exp(-jnp.exp(w))
    a = jax.nn.sigmoid(a0_ref[...] + jnp.dot(lora.astype(BF16), a2_ref[...], preferred_element_type=F32))
    g = jnp.dot(jax.nn.sigmoid(gl).astype(BF16), g2_ref[...], preferred_element_type=F32)
    kk = kx * kk_ref[...]
    kkn = kk / jnp.maximum(jnp.sqrt(_segsum(kk * kk, r2_ref)), 1e-12)
    k = kx * (1.0 + (a - 1.0) * ka_ref[...])
    av = -kkn
    bv = kkn * a
    br = _segsum(bv * r, r2_ref)
    kr = _segsum(k * r, r2_ref)
    a_out[0] = av
    c_out[0] = decay * r + av * br
    w_out[0] = decay
    b_out[0] = bv
    k_out[0] = k
    v_out[0] = vx
    vkr_out[0] = vx * kr
    bon_out[0] = _segsum(r * k * rk_ref[...], r2_ref) * vx
    g_out[0] = g


def _value_diag():
    row = lax.broadcasted_iota(I32, (RW_N, RW_W), 0)
    lane = lax.broadcasted_iota(I32, (RW_N, RW_W), 1)
    return (lane & (RW_N - 1)) == row


def _scan_step(seq_refs, s_ref, t, e_j, zacc, rmat, diag, *, nb, v_terms):
    a_ref, c_ref, w_ref, b_ref, k_ref, v_ref = seq_refs
    width = rmat.shape[0]
    per = (RW_W // width) * RW_N
    row = lambda ref, bb: ref[bb, pl.ds(t, 1), :]
    nterm = 1 + v_terms
    stacked, pz = [], []
    for bb in range(nb):
        s = s_ref[bb]
        v = row(v_ref, bb)
        v_hi = jnp.where(diag, v, 0.0).astype(BF16)
        terms = [(s * row(a_ref, bb)).astype(BF16), v_hi]
        pz.append((s * row(c_ref, bb)).astype(BF16))
        if v_terms == 2:
            terms.append(jnp.where(diag, v - v_hi.astype(F32), 0.0).astype(BF16))
        stacked += [_stack_lane_blocks(x, width) for x in terms]
    red = jnp.dot(jnp.concatenate(stacked, axis=0), rmat, preferred_element_type=F32)
    for bb in range(nb):
        parts = [_unstack_lane_blocks(red[(nterm * bb + i) * per:(nterm * bb + i + 1) * per], RW_N)
                 for i in range(nterm)]
        vb = parts[1] if v_terms == 1 else parts[1] + parts[2]
        s_ref[bb] = s_ref[bb] * row(w_ref, bb) + parts[0] * row(b_ref, bb) + vb * row(k_ref, bb)
    return zacc + jnp.dot(jnp.concatenate(pz, axis=0), e_j, preferred_element_type=F32)


def _rwkv_scan_kernel(*refs, nb, tt, nst, v_terms):
    seq_refs = refs[:6]
    s0_ref, r_ref, e_ref, y_ref, so_ref, s_ref = refs[6:]
    ti = pl.program_id(1)

    @pl.when(ti == 0)
    def _():
        s_ref[...] = s0_ref[...]

    diag = _value_diag()
    rmat = r_ref[...]

    def group(g, carry):
        zacc = jnp.zeros((nb * RW_N, LANES), F32)
        for j in range(nst):
            zacc = _scan_step(seq_refs, s_ref, g * nst + j, e_ref[j], zacc, rmat, diag, nb=nb, v_terms=v_terms)
        for bb in range(nb):
            y_ref[bb, g] = zacc[bb * RW_N:(bb + 1) * RW_N]
        return carry

    lax.fori_loop(0, tt // nst, group, 0)

    @pl.when(ti == pl.num_programs(1) - 1)
    def _():
        so_ref[...] = s_ref[...]


def _rwkv_post_kernel(y_ref, vkr_ref, bon_ref, g_ref, lw_ref, lb_ref, r2_ref, o_ref):
    y = y_ref[...] + vkr_ref[...]
    mu = _segsum(y, r2_ref) * (1.0 / RW_N)
    d = y - mu
    var = _segsum(d * d, r2_ref) * (1.0 / RW_N)
    yn = d * lax.rsqrt(var + RW_LN_EPS) * lw_ref[...] + lb_ref[...]
    o_ref[...] = ((yn + bon_ref[...]) * g_ref[...]).astype(BF16)


def _segment_ones(width):
    return jnp.kron(jnp.eye(width // RW_N, dtype=F32), jnp.ones((RW_N, RW_N), F32)).astype(BF16)


def _segsum_operand():
    seg = _segment_ones(LANES)
    return jnp.concatenate([seg, seg], axis=0)


def rwkv_prep(prw, shift_prev, prm, batch, seq):
    tt = min(seq, 128)
    r2 = _segsum_operand()
    zeros64 = jnp.zeros((RW_N, RW_W), F32)
    w2p = jnp.concatenate([prm["rw_w2"], zeros64], axis=0).astype(BF16)
    a2p = jnp.concatenate([zeros64, prm["rw_a2"]], axis=0).astype(BF16)
    row = lambda a: a.reshape(1, -1)
    blk_in = pl.BlockSpec((1, tt, RW_PROJ), lambda b, i: (b, i, 0))
    blk_out = pl.BlockSpec((1, tt, RW_W), lambda b, i: (b, i, 0))
    c2 = lambda shape: pl.BlockSpec(shape, lambda b, i: (0,) * len(shape))
    return pl.pallas_call(
        functools.partial(_rwkv_prep_kernel, tt=tt),
        grid=(batch, seq // tt),
        in_specs=[blk_in, pl.BlockSpec((1, 1, RW_PROJ), lambda b, i: (b, 0, 0)), c2((1, RW_PROJ)),
                  c2((1, RW_W)), c2((RW_LORA_W, RW_W)), c2((1, RW_W)), c2((RW_LORA_W, RW_W)),
                  c2((RW_GATE_W, RW_W)), c2((1, RW_W)), c2((1, RW_W)), c2((1, RW_W)), c2((2 * LANES, LANES))],
        out_specs=[blk_out] * 9,
        out_shape=[jax.ShapeDtypeStruct((batch, seq, RW_W), F32)] * 9,
        scratch_shapes=[pltpu.VMEM((1, RW_PROJ), F32)],
        compiler_params=_cparams(("parallel", "arbitrary"), 56),
        name="rwkv_prep",
    )(prw.reshape(batch, seq, RW_PROJ), shift_prev.reshape(batch, 1, RW_PROJ), row(prm["rw_mu"]),
      row(prm["rw_w0"]), w2p, row(prm["rw_a0"]), a2p, prm["rw_g2"].astype(BF16), row(prm["rw_k_k"]),
      row(prm["rw_k_a"]), row(prm["rw_r_k"]), r2)


SCAN_NB = 4


def _scan_group(seq):
    return min(seq, LANES // RW_HEADS)


def _scan_operands(wkv0, batch, seq):
    nst = _scan_group(seq)
    head = jnp.arange(RW_W, dtype=I32) // RW_N
    ecomp = (head[None, :, None] + RW_HEADS * jnp.arange(nst, dtype=I32)[:, None, None]
             == jnp.arange(LANES, dtype=I32)[None, None, :]).astype(BF16)
    s0 = wkv0.astype(F32).transpose(0, 2, 1, 3).reshape(batch, RW_N, RW_W)
    return s0, _segment_ones(MXU_DIM), ecomp


def _scan_out_shapes(batch, seq):
    return [jax.ShapeDtypeStruct((batch, seq // _scan_group(seq), RW_N, LANES), F32),
            jax.ShapeDtypeStruct((batch, RW_N, RW_W), F32)]


def rwkv_scan(seq_arrays, wkv0, batch, seq):
    nst = _scan_group(seq)
    ts = min(seq, 64)
    nb = SCAN_NB
    v_terms = 2 if seq == 1 else 1
    c2 = lambda shape: pl.BlockSpec(shape, lambda b, i: (0,) * len(shape))
    seq_blk = pl.BlockSpec((nb, ts, RW_W), lambda b, i: (b, i, 0))
    grp_blk = pl.BlockSpec((nb, ts // nst, RW_N, LANES), lambda b, i: (b, i, 0, 0))
    st_blk = pl.BlockSpec((nb, RW_N, RW_W), lambda b, i: (b, 0, 0))
    return pl.pallas_call(
        functools.partial(_rwkv_scan_kernel, nb=nb, tt=ts, nst=nst, v_terms=v_terms),
        grid=(batch // nb, seq // ts),
        in_specs=[seq_blk] * 6 + [st_blk, c2((MXU_DIM, MXU_DIM)), c2((nst, RW_W, LANES))],
        out_specs=[grp_blk, st_blk],
        out_shape=_scan_out_shapes(batch, seq),
        scratch_shapes=[pltpu.VMEM((nb, RW_N, RW_W), F32)],
        compiler_params=_cparams(("parallel", "arbitrary"), 48),
        name="rwkv_scan",
    )(*seq_arrays, *_scan_operands(wkv0, batch, seq))


def rwkv_finish(yc, s_fin, vkr, bon, gv, prm, batch, seq):
    m = batch * seq
    nst = _scan_group(seq)
    row = lambda a: a.reshape(1, -1)
    yz = yc.reshape(batch, seq // nst, RW_N, LANES // RW_HEADS, RW_HEADS)[:, :, :, :nst, :]
    yz = yz.transpose(0, 1, 3, 4, 2).reshape(m, RW_W)
    wkv_new = s_fin.reshape(batch, RW_N, RW_HEADS, RW_N).transpose(0, 2, 1, 3)

    tm = min(m, ROW_TILE)
    rows = pl.BlockSpec((tm, RW_W), lambda i: (i, 0))
    y = pl.pallas_call(
        _rwkv_post_kernel,
        grid=(m // tm,),
        in_specs=[rows] * 4 + [_full((1, RW_W)), _full((1, RW_W)), _full((2 * LANES, LANES))],
        out_specs=rows,
        out_shape=jax.ShapeDtypeStruct((m, RW_W), BF16),
        compiler_params=_cparams(("parallel",), 48),
        name="rwkv_post",
    )(yz, vkr.reshape(m, RW_W), bon.reshape(m, RW_W), gv.reshape(m, RW_W), row(prm["rw_ln_w"]),
      row(prm["rw_ln_b"]), _segsum_operand())
    return y, wkv_new


def rwkv_mix(prw, shift_prev, wkv0, prm, batch, seq):
    *seq_arrays, vkr, bon, gv = rwkv_prep(prw, shift_prev, prm, batch, seq)
    yc, s_fin = rwkv_scan(seq_arrays[:6], wkv0, batch, seq)
    return rwkv_finish(yc, s_fin, vkr, bon, gv, prm, batch, seq)


def _merge_kernel(ya_ref, yb_ref, ga_ref, gb_ref, ba_ref, bb_ref, wa_ref, wb_ref, o_ref):
    pa = jnp.dot(ya_ref[...].astype(BF16), wa_ref[...], preferred_element_type=F32)
    pb = jnp.dot(yb_ref[...], wb_ref[...], preferred_element_type=F32)
    ga = jax.nn.sigmoid(ga_ref[...] + ba_ref[...])
    gb = jax.nn.sigmoid(gb_ref[...] + bb_ref[...])
    o_ref[...] = (ga * pa + gb * pb).astype(BF16)


def merge_branches(y_att, y_rw, g_lin, b_gate, wa, wb):
    m = y_att.shape[0]
    tm = min(m, ROW_TILE)
    d = D_MODEL
    return pl.pallas_call(
        _merge_kernel,
        grid=(m // tm,),
        in_specs=[pl.BlockSpec((tm, ATT_W), lambda i: (i, 0)), pl.BlockSpec((tm, RW_W), lambda i: (i, 0)),
                  pl.BlockSpec((tm, d), lambda i: (i, 0)), pl.BlockSpec((tm, d), lambda i: (i, 1)),
                  pl.BlockSpec((1, d), lambda i: (0, 0)), pl.BlockSpec((1, d), lambda i: (0, 1)),
                  _full((ATT_W, d)), _full((RW_W, d))],
        out_specs=pl.BlockSpec((tm, d), lambda i: (i, 0)),
        out_shape=jax.ShapeDtypeStruct((m, d), BF16),
        compiler_params=_cparams(("parallel",), 48),
        name="merge_branches",
    )(y_att, y_rw, g_lin, g_lin, b_gate.reshape(1, 2 * d), b_gate.reshape(1, 2 * d), wa, wb)


SLAB = D_MODEL // LANES


def _store_token_slabs(ref, x):
    n = x.shape[0]
    for c in range(SLAB):
        ref[pl.ds(c, n, stride=SLAB), :] = x[:, c * LANES:(c + 1) * LANES]


def _load_token_slabs(ref, n):
    return jnp.concatenate([ref[pl.ds(c, n, stride=SLAB), :] for c in range(SLAB)], axis=1)


def _outproj_router_kernel(mg_ref, wo_ref, x_ref, nw_ref, wr_ref, br_ref, h_ref, hn_ref, eid_ref, ew_ref):
    h = x_ref[...] + jnp.dot(mg_ref[...], wo_ref[...], preferred_element_type=F32)
    h_ref[...] = h
    ms = jnp.mean(h * h, axis=-1, keepdims=True)
    hn = h * lax.rsqrt(ms + NORM_EPS) * nw_ref[...]
    _store_token_slabs(hn_ref, hn)
    lg = jnp.dot(hn.astype(BF16), wr_ref[...], preferred_element_type=F32) + br_ref[...]
    lane = lax.broadcasted_iota(I32, lg.shape, 1)
    lane_f = lane.astype(F32)
    big = float(LANES)
    lgm = jnp.where(lane < N_GROUPS, lg, NEG)
    gmax = jnp.max(lgm, axis=-1, keepdims=True)
    gidx = jnp.min(jnp.where(lgm == gmax, lane_f, big), axis=-1, keepdims=True)
    g_w = 1.0 / jnp.sum(jnp.exp(lgm - gmax), axis=-1, keepdims=True)
    lane_group = ((lane >> 3) - 1).astype(F32)
    in_group = (lane >= N_GROUPS) & (lane < N_GROUPS + N_EXPERTS) & (lane_group == gidx)
    le = jnp.where(in_group, lg, NEG)
    m1 = jnp.max(le, axis=-1, keepdims=True)
    i1 = jnp.min(jnp.where(le == m1, lane_f, big), axis=-1, keepdims=True)
    le2 = jnp.where(lane_f == i1, NEG, le)
    m2 = jnp.max(le2, axis=-1, keepdims=True)
    i2 = jnp.min(jnp.where(le2 == m2, lane_f, big), axis=-1, keepdims=True)
    e2 = jnp.exp(m2 - m1)
    w1 = g_w / (1.0 + e2)
    w2 = g_w * e2 / (1.0 + e2)
    col = lax.broadcasted_iota(I32, eid_ref.shape, 1)
    eid_ref[...] = jnp.where(col == 0, i1, i2).astype(I32) - N_GROUPS
    ew_ref[...] = jnp.where(col == 0, w1, w2)


def outproj_router(merged, w_out, x, norm_w, w_router, b_router):
    m, d = x.shape
    tm = min(m, ROW_TILE)
    rows = lambda w: pl.BlockSpec((tm, w), lambda i: (i, 0))
    return pl.pallas_call(
        _outproj_router_kernel,
        grid=(m // tm,),
        in_specs=[rows(d), _full((d, d)), rows(d), _full((1, d)), _full((d, LANES)), _full((1, LANES))],
        out_specs=[rows(d), pl.BlockSpec((tm * SLAB, LANES), lambda i: (i, 0)), rows(2), rows(2)],
        out_shape=[jax.ShapeDtypeStruct((m, d), F32), jax.ShapeDtypeStruct((m * SLAB, LANES), F32),
                   jax.ShapeDtypeStruct((m, 2), I32), jax.ShapeDtypeStruct((m, 2), F32)],
        compiler_params=_cparams(("parallel",), 48),
        name="outproj_router",
    )(merged, w_out, x, norm_w.reshape(1, d), w_router, b_router)


def _moe_kernel(starts_ref, counts_ref, pcounts_ref, src_ref, hp_ref, hs_ref, wg_ref, wu_ref, wd_ref,
                yp_ref, ys_ref, wgb, wub, wdb, xbuf, ybuf, sem_in, sem_out, *, n_prompt, n_sample):
    e = pl.program_id(0)

    @pl.when(e == 0)
    def _():
        xbuf[...] = jnp.zeros(xbuf.shape, F32)

    n = counts_ref[e]
    n_p = pcounts_ref[e]
    start = starts_ref[e]
    nblk = (n + MOE_BLK - 1) // MOE_BLK

    def slab(i):
        return pl.ds(pl.multiple_of(i * SLAB, SLAB), SLAB)

    def row_in(src_hbm, tok, slot, r):
        return pltpu.make_async_copy(src_hbm.at[slab(tok)], xbuf.at[slot, slab(r)], sem_in.at[slot])

    def row_out(dst_hbm, dst, r):
        return pltpu.make_async_copy(ybuf.at[slab(r)], dst_hbm.at[slab(dst)], sem_out)

    def for_rows(lo, hi, body):
        groups = (hi - lo) // MOE_UNROLL

        def group(i, c):
            for u in range(MOE_UNROLL):
                body(lo + i * MOE_UNROLL + u)
            return c

        lax.fori_loop(0, groups, group, 0)
        lax.fori_loop(lo + groups * MOE_UNROLL, hi, lambda r, c: (body(r), c)[1], 0)

    def block_rows(bi):
        nv = jnp.minimum(MOE_BLK, n - bi * MOE_BLK)
        nvp = jnp.clip(n_p - bi * MOE_BLK, 0, nv)
        return start + bi * MOE_BLK, nv, nvp

    def gather_start(bi, slot):
        r0, nv, nvp = block_rows(bi)
        for_rows(0, nvp, lambda r: row_in(hp_ref, src_ref[r0 + r] >> 1, slot, r).start())
        for_rows(nvp, nv, lambda r: row_in(hs_ref, (src_ref[r0 + r] >> 1) - n_prompt, slot, r).start())

    def gather_wait(bi, slot):
        _, nv, _ = block_rows(bi)
        for_rows(0, nv, lambda r: row_in(hp_ref, 0, slot, r).wait())

    def scatter_start(bi):
        r0, nv, nvp = block_rows(bi)

        def to_prompt(r):
            src = src_ref[r0 + r]
            row_out(yp_ref, (src & 1) * n_prompt + (src >> 1), r).start()

        def to_sample(r):
            src = src_ref[r0 + r]
            row_out(ys_ref, (src & 1) * n_sample + (src >> 1) - n_prompt, r).start()

        for_rows(0, nvp, to_prompt)
        for_rows(nvp, nv, to_sample)

    def scatter_wait(bi):
        _, nv, _ = block_rows(bi)
        for_rows(0, nv, lambda r: row_out(yp_ref, 0, r).wait())

    @pl.when(n > 0)
    def _():
        gather_start(0, 0)
        wgb[...] = wg_ref[0].astype(BF16)
        wub[...] = wu_ref[0].astype(BF16)
        wdb[...] = wd_ref[0].astype(BF16)

        def block(bi, carry):
            slot = bi & 1
            gather_wait(bi, slot)

            @pl.when(bi + 1 < nblk)
            def _():
                gather_start(bi + 1, 1 - slot)

            x = _load_token_slabs(xbuf.at[slot], MOE_BLK).astype(BF16)
            hg = jnp.dot(x, wgb[...], preferred_element_type=F32)
            hu = jnp.dot(x, wub[...], preferred_element_type=F32)
            hd = (hg * jax.nn.sigmoid(hg) * hu).astype(BF16)
            y = jnp.dot(hd, wdb[...], preferred_element_type=F32)

            @pl.when(bi > 0)
            def _():
                scatter_wait(bi - 1)

            _store_token_slabs(ybuf, y)
            scatter_start(bi)
            return carry

        lax.fori_loop(0, nblk, block, 0)
        scatter_wait(nblk - 1)


def moe_experts(hn_p, hn_s, eid, w_gate, w_up, w_down):
    d = D_MODEL
    n_prompt = hn_p.shape[0] // SLAB
    n_sample = hn_s.shape[0] // SLAB
    n_rows = eid.shape[0] * 2
    flat_e = eid.reshape(n_rows)
    onehot = (flat_e[:, None] == jnp.arange(N_EXPERTS, dtype=I32)[None, :]).astype(I32)
    counts = jnp.sum(onehot, axis=0)
    pcounts = jnp.sum(onehot[:2 * n_prompt], axis=0)
    starts = jnp.cumsum(counts) - counts
    rank = jnp.sum(onehot * (jnp.cumsum(onehot, axis=0) - 1), axis=1)
    pos = starts[flat_e] + rank
    src = jnp.zeros((n_rows,), I32).at[pos].set(jnp.arange(n_rows, dtype=I32), unique_indices=True,
                                                mode="promise_in_bounds")
    any_spec = pl.BlockSpec(memory_space=pl.ANY)
    grid_spec = pltpu.PrefetchScalarGridSpec(
        num_scalar_prefetch=4,
        grid=(N_EXPERTS,),
        in_specs=[any_spec, any_spec,
                  pl.BlockSpec((1, d, D_FF), lambda e, *_: (e, 0, 0)),
                  pl.BlockSpec((1, d, D_FF), lambda e, *_: (e, 0, 0)),
                  pl.BlockSpec((1, D_FF, d), lambda e, *_: (e, 0, 0))],
        out_specs=[any_spec, any_spec],
        scratch_shapes=[pltpu.VMEM((d, D_FF), BF16), pltpu.VMEM((d, D_FF), BF16), pltpu.VMEM((D_FF, d), BF16),
                        pltpu.VMEM((2, MOE_BLK * SLAB, LANES), F32), pltpu.VMEM((MOE_BLK * SLAB, LANES), F32),
                        pltpu.SemaphoreType.DMA((2,)), pltpu.SemaphoreType.DMA(())],
    )
    return pl.pallas_call(
        functools.partial(_moe_kernel, n_prompt=n_prompt, n_sample=n_sample),
        grid_spec=grid_spec,
        out_shape=[jax.ShapeDtypeStruct((2 * n_prompt * SLAB, LANES), F32),
                   jax.ShapeDtypeStruct((2 * n_sample * SLAB, LANES), F32)],
        compiler_params=_cparams(("arbitrary",), 56),
        name="moe_experts",
    )(starts, counts, pcounts, src, hn_p, hn_s, w_gate, w_up, w_down)


def _final_kernel(h_ref, y0_ref, y1_ref, ew_ref, nw_ref, o_ref):
    ew = ew_ref[...]
    tm = h_ref.shape[0]
    x = (h_ref[...] + ew[:, 0:1] * _load_token_slabs(y0_ref, tm) + ew[:, 1:2] * _load_token_slabs(y1_ref, tm))
    ms = jnp.mean(x * x, axis=-1, keepdims=True)
    o_ref[...] = x * lax.rsqrt(ms + NORM_EPS) * nw_ref[...]


def combine_final(h, y_rows, ew, norm_w):
    m, d = h.shape
    tm = min(m, ROW_TILE)
    o1 = m // tm
    return pl.pallas_call(
        _final_kernel,
        grid=(m // tm,),
        in_specs=[pl.BlockSpec((tm, d), lambda i: (i, 0)),
                  pl.BlockSpec((tm * SLAB, LANES), lambda i: (i, 0)),
                  pl.BlockSpec((tm * SLAB, LANES), lambda i: (i + o1, 0)),
                  pl.BlockSpec((tm, 2), lambda i: (i, 0)), _full((1, d))],
        out_specs=pl.BlockSpec((tm, d), lambda i: (i, 0)),
        out_shape=jax.ShapeDtypeStruct((m, d), F32),
        compiler_params=_cparams(("parallel",), 48),
        name="combine_final",
    )(h, y_rows, y_rows, ew, norm_w.reshape(1, d))


_RW_COL0 = 3 * ATT_W
_GATE_COL0 = _RW_COL0 + RW_PROJ


def _in_proj(x2, prm, q_dtype):
    xn = rmsnorm_bf16(x2, prm["norm_mix_w"])
    w_in = prm["w_in"]
    q = matmul_cols(xn, w_in, 0, ATT_W, q_dtype)
    k = matmul_cols(xn, w_in, ATT_W, ATT_W, F32)
    v = matmul_cols(xn, w_in, 2 * ATT_W, ATT_W, F32)
    prw = matmul_cols(xn, w_in, _RW_COL0, RW_PROJ, F32)
    g_lin = matmul_cols(xn, w_in, _GATE_COL0, 2 * D_MODEL, F32)
    return q, k, v, prw, g_lin


def _mixer_tail(x2, y_att, y_rw, g_lin, prm):
    merged = merge_branches(y_att, y_rw, g_lin, prm["b_gate"], prm["wa_bf16"], prm["wb_bf16"])
    return outproj_router(merged, prm["wo_bf16"], x2, prm["norm_ffn_w"], prm["w_router"], prm["b_router"])


def kernel(x_prompt, x_sample, cache_k, cache_v, state_wkv, state_shift, page_table, norm_mix_w, w_in, b_gate, lambda_q1, lambda_k1, lambda_q2, lambda_k2, subln_w, rel_bias, rw_mu, rw_w0, rw_w2, rw_a0, rw_a2, rw_g2, rw_k_k, rw_k_a, rw_r_k, rw_ln_w, rw_ln_b, w_branch_a, w_branch_b, w_out, norm_ffn_w, w_router_g, b_router_g, w_router_e, b_router_e, w_gate_e, w_up_e, w_down_e, norm_final_w):
    assert w_in.shape[0] == 1, "single-layer trunk"
    bp, sp, d = x_prompt.shape
    bs, ss, _ = x_sample.shape
    assert ss == 1 and d == D_MODEL
    n_pool = cache_k.shape[1]
    pad_lanes = LANES - N_GROUPS - N_EXPERTS
    prm = {
        "norm_mix_w": norm_mix_w[0], "w_in": w_in[0], "b_gate": b_gate[0],
        "rw_mu": rw_mu[0], "rw_w0": rw_w0[0], "rw_w2": rw_w2[0], "rw_a0": rw_a0[0], "rw_a2": rw_a2[0],
        "rw_g2": rw_g2[0], "rw_k_k": rw_k_k[0], "rw_k_a": rw_k_a[0], "rw_r_k": rw_r_k[0].reshape(RW_W),
        "rw_ln_w": rw_ln_w[0], "rw_ln_b": rw_ln_b[0],
        "wa_bf16": w_branch_a[0].astype(BF16), "wb_bf16": w_branch_b[0].astype(BF16),
        "wo_bf16": w_out[0].astype(BF16), "norm_ffn_w": norm_ffn_w[0],
        "w_router": jnp.concatenate([w_router_g[0], w_router_e[0], jnp.zeros((d, pad_lanes), F32)],
                                    axis=1).astype(BF16),
        "b_router": jnp.concatenate([b_router_g[0], b_router_e[0], jnp.zeros((pad_lanes,), F32)]).reshape(1, LANES),
    }
    lamv = jnp.stack([lambda_q1[0], lambda_k1[0], lambda_q2[0], lambda_k2[0]]).astype(F32)
    sub_w = subln_w[0]

    mp = bp * sp
    xp2 = x_prompt.reshape(mp, d)
    q_p, k_p, v_p, prw_p, g_p = _in_proj(xp2, prm, BF16)
    att_p = attn_prompt(q_p, k_p, v_p, lamv, rel_bias, sub_w, bp, sp)
    rw_p, wkv_p = rwkv_mix(prw_p, jnp.zeros((bp, RW_PROJ), F32), jnp.zeros((bp, RW_HEADS, RW_N, RW_N), F32),
                           prm, bp, sp)
    h_p, hn_p, eid_p, ew_p = _mixer_tail(xp2, att_p, rw_p, g_p, prm)

    xs2 = x_sample.reshape(bs, d)
    q_s, k_s, v_s, prw_s, g_s = _in_proj(xs2, prm, F32)
    heads = lambda a: a.reshape(bs, ATT_HEADS, HEAD_W)
    att_s = attn_decode(heads(q_s), heads(k_s), heads(v_s),
                        cache_k[0].reshape(n_pool, PAGE * ATT_HEADS, HEAD_W),
                        cache_v[0].reshape(n_pool, PAGE * ATT_HEADS, HEAD_W),
                        page_table, lamv, rel_bias, sub_w).reshape(bs, ATT_W)
    rw_s, wkv_s = rwkv_mix(prw_s, state_shift[0], state_wkv[0], prm, bs, 1)
    h_s, hn_s, eid_s, ew_s = _mixer_tail(xs2, att_s, rw_s, g_s, prm)

    yr_p, yr_s = moe_experts(hn_p, hn_s, jnp.concatenate([eid_p, eid_s], axis=0),
                             w_gate_e[0], w_up_e[0], w_down_e[0])
    y_p = combine_final(h_p, yr_p, ew_p, norm_final_w)
    y_s = combine_final(h_s, yr_s, ew_s, norm_final_w)

    return (y_p.reshape(bp, sp, d), y_s.reshape(bs, 1, d),
            k_p.reshape(1, bp, sp, ATT_HEADS, HEAD_W), v_p.reshape(1, bp, sp, ATT_HEADS, HEAD_W),
            wkv_p.astype(state_wkv.dtype)[None], prw_p.reshape(bp, sp, RW_PROJ)[None, :, -1],
            k_s.reshape(1, bs, 1, ATT_HEADS, HEAD_W), v_s.reshape(1, bs, 1, ATT_HEADS, HEAD_W),
            wkv_s.astype(state_wkv.dtype)[None], prw_s.reshape(1, bs, RW_PROJ))
```

```python
import functools
import math

import jax
import jax.numpy as jnp
from jax import lax
from jax.experimental import pallas as pl
from jax.experimental.pallas import tpu as pltpu

F32 = jnp.float32
BF16 = jnp.bfloat16
I32 = jnp.int32

D_MODEL = 2048
ATT_HEADS = 8
ATT_DK = 64
HEAD_W = 2 * ATT_DK
ATT_W = ATT_HEADS * HEAD_W
ATT_SCALE = ATT_DK ** -0.5
RW_HEADS = 16
RW_N = 64
RW_W = RW_HEADS * RW_N
RW_LORA_W = 128
RW_GATE_W = 128
RW_PROJ = 3 * RW_W + RW_LORA_W + RW_GATE_W
N_BUCKETS = 32
MAX_DISTANCE = 128
PAGE = 128
EXPERTS_PER_GROUP = 8
N_GROUPS = 8
N_EXPERTS = N_GROUPS * EXPERTS_PER_GROUP
D_FF = 512
NORM_EPS = 1e-6
SUBLN_EPS = 1e-5
RW_LN_EPS = 64e-5
LAM_INIT = 0.8 - 0.6 * math.exp(-0.3 * 0)
NEG = -1e30
LANES = 128
MXU_DIM = 256
MOE_BLK = 384
MOE_UNROLL = 16
ROW_TILE = 512
MIB = 1024 * 1024


def _cparams(sem, vmem_mib):
    return pltpu.CompilerParams(dimension_semantics=sem, vmem_limit_bytes=vmem_mib * MIB)


def _full(shape):
    nd = len(shape)
    return pl.BlockSpec(shape, lambda *_: (0,) * nd)


def _rmsnorm_bf16_kernel(x_ref, w_ref, o_ref):
    x = x_ref[...]
    ms = jnp.mean(x * x, axis=-1, keepdims=True)
    o_ref[...] = (x * lax.rsqrt(ms + NORM_EPS) * w_ref[...]).astype(BF16)


def rmsnorm_bf16(x, w):
    m, d = x.shape
    tm = min(m, 512)
    return pl.pallas_call(
        _rmsnorm_bf16_kernel,
        grid=(m // tm,),
        in_specs=[pl.BlockSpec((tm, d), lambda i: (i, 0)), _full((1, d))],
        out_specs=pl.BlockSpec((tm, d), lambda i: (i, 0)),
        out_shape=jax.ShapeDtypeStruct((m, d), BF16),
        compiler_params=_cparams(("parallel",), 32),
        name="rmsnorm_bf16",
    )(x, w.reshape(1, d))


def _matmul_kernel(x_ref, w_ref, o_ref):
    o_ref[...] = jnp.dot(x_ref[...], w_ref[...].astype(BF16),
                         preferred_element_type=F32).astype(o_ref.dtype)


def matmul_cols(x, w, col0, ncols, out_dtype):
    m, k = x.shape
    tm = next(t for t in (2048, 1024, 512, 256, m) if m % t == 0)
    tn = next(t for t in (512, 256, 128) if col0 % t == 0 and ncols % t == 0)
    off = col0 // tn
    return pl.pallas_call(
        _matmul_kernel,
        grid=(m // tm, ncols // tn),
        in_specs=[pl.BlockSpec((tm, k), lambda i, j: (i, 0)),
                  pl.BlockSpec((k, tn), lambda i, j: (0, j + off))],
        out_specs=pl.BlockSpec((tm, tn), lambda i, j: (i, j)),
        out_shape=jax.ShapeDtypeStruct((m, ncols), out_dtype),
        compiler_params=_cparams(("parallel", "arbitrary"), 48),
        name="in_proj",
    )(x, w)


def _rel_bucket(dist):
    n = jnp.maximum(dist, 0)
    max_exact = N_BUCKETS // 2
    large = max_exact + (jnp.log(jnp.maximum(n, 1).astype(F32) / max_exact)
                         / math.log(MAX_DISTANCE / max_exact) * (N_BUCKETS - max_exact)).astype(I32)
    large = jnp.minimum(large, N_BUCKETS - 1)
    return jnp.where(n < max_exact, n, large)


def _bias_by_distance(rel_bias, n):
    d = jnp.arange(n, dtype=I32)
    table = rel_bias.astype(F32)[_rel_bucket(d)]
    far = rel_bias.astype(F32)[N_BUCKETS - 1]
    return (table - far[None, :]).T


def _toeplitz(vals, t):
    h = vals.shape[0]
    w = jnp.concatenate([vals[:, ::-1], jnp.zeros((h, 1), vals.dtype)], axis=1)
    flat = jnp.tile(w, (1, t))[:, :t * (2 * t - 1)]
    return flat.reshape(h, t, 2 * t - 1)[:, :, t - 1:]


def _lambda(lamv_ref):
    lv = lamv_ref[...]
    s1 = jnp.sum(lv[0:1] * lv[1:2], axis=-1, keepdims=True)
    s2 = jnp.sum(lv[2:3] * lv[3:4], axis=-1, keepdims=True)
    return jnp.exp(s1) - jnp.exp(s2) + LAM_INIT


def _split_halves(q):
    lane = lax.broadcasted_iota(I32, q.shape, 1)
    return jnp.concatenate([jnp.where(lane < ATT_DK, q, 0.0), jnp.where(lane >= ATT_DK, q, 0.0)], axis=0)


def _attn_prompt_kernel(lamv_ref, q_ref, k_ref, v_ref, bd_ref, bp_ref, sw_ref, o_ref, kb_ref, v1_ref, *, t, nq):
    kb_ref[...] = k_ref[...].astype(BF16)
    v1_ref[:, :HEAD_W] = v_ref[...].astype(BF16)
    v1_ref[:, HEAD_W:] = jnp.ones((v1_ref.shape[0], HEAD_W), BF16)
    lam = _lambda(lamv_ref)
    sw = sw_ref[...] * (1.0 - LAM_INIT)
    bd = bd_ref[0]
    bp = bp_ref[0]
    bd2 = jnp.concatenate([bd, bd], axis=0)
    bp2 = jnp.concatenate([bp, bp], axis=0)

    for i in range(nq):
        n_keys = (i + 1) * t
        q = q_ref[i * t:(i + 1) * t, :].astype(F32) * ATT_SCALE
        q2 = _split_halves(q).astype(BF16)
        s = lax.dot_general(q2, kb_ref[:n_keys, :], (((1,), (1,)), ((), ())), preferred_element_type=F32)
        near = [s[:, i * t:] + bd2]
        if i >= 1:
            near.insert(0, s[:, (i - 1) * t:i * t] + bp2)
        s = jnp.concatenate(([s[:, :(i - 1) * t]] if i >= 2 else []) + near, axis=1)
        e = jnp.exp(s - jnp.max(s, axis=-1, keepdims=True)).astype(BF16)
        a = jnp.dot(e, v1_ref[:n_keys, :], preferred_element_type=F32)
        a = a[:, :HEAD_W] / a[:, HEAD_W:]
        o = a[:t] - lam * a[t:]
        ms = jnp.mean(o * o, axis=-1, keepdims=True)
        o_ref[i * t:(i + 1) * t, :] = o * lax.rsqrt(ms + SUBLN_EPS) * sw


def attn_prompt(q, k, v, lamv, rel_bias, subln_w, batch, seq):
    t = min(seq, 256)
    nq = seq // t
    assert t >= LANES, "blocks two or more away must lie beyond the last distinct distance bucket"
    bd0 = _bias_by_distance(rel_bias, 2 * t)
    bdiag = _toeplitz(jnp.concatenate([jnp.full((ATT_HEADS, t - 1), NEG, F32), bd0[:, :t]], axis=1), t)
    bprev = _toeplitz(bd0[:, 1:], t)
    blk = lambda: pl.BlockSpec((seq, HEAD_W), lambda b, h: (b, h))
    return pl.pallas_call(
        functools.partial(_attn_prompt_kernel, t=t, nq=nq),
        grid=(batch, ATT_HEADS),
        in_specs=[_full((4, ATT_DK)), blk(), blk(), blk(),
                  pl.BlockSpec((1, t, t), lambda b, h: (h, 0, 0)),
                  pl.BlockSpec((1, t, t), lambda b, h: (h, 0, 0)),
                  _full((1, HEAD_W))],
        out_specs=blk(),
        out_shape=jax.ShapeDtypeStruct((batch * seq, ATT_W), F32),
        scratch_shapes=[pltpu.VMEM((seq, HEAD_W), BF16), pltpu.VMEM((seq, 2 * HEAD_W), BF16)],
        compiler_params=_cparams(("parallel", "parallel"), 48),
        name="attn_prompt",
    )(lamv, q, k, v, bdiag, bprev, subln_w.reshape(1, HEAD_W))


def _attn_decode_kernel(pt_ref, lamv_ref, q_ref, kn_ref, vn_ref, bias_ref, bnew_ref, sw_ref, *rest, pages):
    del pt_ref
    k_refs = rest[:pages]
    v_refs = rest[pages:2 * pages]
    o_ref, m_ref, l_ref, acc_ref = rest[2 * pages:]
    c = pl.program_id(1)
    last = pl.num_programs(1) - 1

    @pl.when(c == 0)
    def _():
        m_ref[...] = jnp.full(m_ref.shape, NEG, F32)
        l_ref[...] = jnp.zeros(l_ref.shape, F32)
        acc_ref[...] = jnp.zeros(acc_ref.shape, F32)

    q16 = _split_halves(q_ref[0] * ATT_SCALE)
    row_head = lax.broadcasted_iota(I32, (2 * ATT_HEADS, 1), 0) & (ATT_HEADS - 1)
    qh = [jnp.where(row_head == h, q16, 0.0).astype(BF16) for h in range(ATT_HEADS)]
    pairs = pages // 2
    st = {"s": [None] * pairs, "pv": jnp.zeros((2 * ATT_HEADS, HEAD_W), F32)}

    def head_rows(ref, h):
        return ref[0, pl.ds(h, PAGE, stride=ATT_HEADS), :]

    def two_pages(refs, pp, h):
        return jnp.concatenate([head_rows(refs[2 * pp], h), head_rows(refs[2 * pp + 1], h)], axis=0).astype(BF16)

    def logits(pp):
        sp = None
        for h in range(ATT_HEADS):
            d = lax.dot_general(qh[h], two_pages(k_refs, pp, h), (((1,), (1,)), ((), ())),
                                preferred_element_type=F32)
            sp = d if sp is None else sp + d
        st["s"][pp] = sp

    def softmax_update():
        s = jnp.concatenate(st["s"], axis=1)
        s = s + bias_ref[...] * (c == last).astype(F32)
        m = m_ref[...]
        mn = jnp.maximum(m, jnp.max(s, axis=-1, keepdims=True))
        st["alpha"] = jnp.exp(m - mn)
        e = jnp.exp(s - mn)
        l_ref[...] = st["alpha"] * l_ref[...] + jnp.sum(e, axis=-1, keepdims=True)
        m_ref[...] = mn
        st["e"] = e.astype(BF16)

    def weighted_values(pp):
        ep = st["e"][:, pp * 2 * PAGE:(pp + 1) * 2 * PAGE]
        for h in range(ATT_HEADS):
            r = jnp.dot(ep, two_pages(v_refs, pp, h), preferred_element_type=F32)
            st["pv"] = st["pv"] + jnp.where(row_head == h, r, 0.0)

    def accumulate():
        acc_ref[...] = st["alpha"] * acc_ref[...] + st["pv"]

    for pp in range(pairs):
        logits(pp)
    softmax_update()
    for pp in range(pairs):
        weighted_values(pp)
    accumulate()

    @pl.when(c == last)
    def _():
        lam = _lambda(lamv_ref)
        m0 = m_ref[...]
        k16 = jnp.concatenate([kn_ref[0], kn_ref[0]], axis=0)
        v16 = jnp.concatenate([vn_ref[0], vn_ref[0]], axis=0)
        s_new = jnp.sum(q16 * k16, axis=-1, keepdims=True) + bnew_ref[...]
        m1 = jnp.maximum(m0, s_new)
        a1 = jnp.exp(m0 - m1)
        e_new = jnp.exp(s_new - m1)
        l1 = a1 * l_ref[...] + e_new
        acc1 = a1 * acc_ref[...] + e_new * v16
        o = acc1[:ATT_HEADS] / l1[:ATT_HEADS] - lam * (acc1[ATT_HEADS:] / l1[ATT_HEADS:])
        ms = jnp.mean(o * o, axis=-1, keepdims=True)
        o_ref[0] = o * lax.rsqrt(ms + SUBLN_EPS) * (sw_ref[...] * (1.0 - LAM_INIT))


def attn_decode(q, k_new, v_new, cache_k, cache_v, page_table, lamv, rel_bias, subln_w):
    nb, n_pages = page_table.shape
    pages = next(g for g in (8, 4, 2) if n_pages % g == 0)
    n_chunks = n_pages // pages
    width = pages * PAGE
    bd0 = _bias_by_distance(rel_bias, width + 1)
    near = bd0[:, width - jnp.arange(width, dtype=I32)]
    bias_last = jnp.concatenate([near, near], axis=0)
    bnew = jnp.concatenate([bd0[:, 0:1], bd0[:, 0:1]], axis=0)
    row = lambda: pl.BlockSpec((1, ATT_HEADS, HEAD_W), lambda b, c, pt: (b, 0, 0))
    page = lambda g: pl.BlockSpec((1, PAGE * ATT_HEADS, HEAD_W), lambda b, c, pt: (pt[b, c * pages + g], 0, 0))
    const = lambda shape: pl.BlockSpec(shape, lambda b, c, pt: (0,) * len(shape))
    grid_spec = pltpu.PrefetchScalarGridSpec(
        num_scalar_prefetch=1,
        grid=(nb, n_chunks),
        in_specs=[const((4, ATT_DK)), row(), row(), row(), const((2 * ATT_HEADS, width)),
                  const((2 * ATT_HEADS, 1)), const((1, HEAD_W))]
                 + [page(g) for g in range(pages)] + [page(g) for g in range(pages)],
        out_specs=row(),
        scratch_shapes=[pltpu.VMEM((2 * ATT_HEADS, 1), F32), pltpu.VMEM((2 * ATT_HEADS, 1), F32),
                        pltpu.VMEM((2 * ATT_HEADS, HEAD_W), F32)],
    )
    return pl.pallas_call(
        functools.partial(_attn_decode_kernel, pages=pages),
        grid_spec=grid_spec,
        out_shape=jax.ShapeDtypeStruct((nb, ATT_HEADS, HEAD_W), F32),
        compiler_params=_cparams(("parallel", "arbitrary"), 56),
        name="attn_decode",
    )(page_table, lamv, q, k_new, v_new, bias_last, bnew, subln_w.reshape(1, HEAD_W),
      *([cache_k] * pages), *([cache_v] * pages))


def _stack_lane_blocks(x, width=LANES):
    return jnp.concatenate([x[:, c * width:(c + 1) * width] for c in range(x.shape[1] // width)], axis=0)


def _unstack_lane_blocks(y, n):
    return jnp.concatenate([y[c * n:(c + 1) * n] for c in range(y.shape[0] // n)], axis=1)


def _segsum(x, r2_ref):
    n = x.shape[0]
    xs = _stack_lane_blocks(x)
    hi = xs.astype(BF16)
    lo = (xs - hi.astype(F32)).astype(BF16)
    y = jnp.dot(jnp.concatenate([hi, lo], axis=1), r2_ref[...], preferred_element_type=F32)
    return _unstack_lane_blocks(y, n)


def _rwkv_prep_kernel(p_ref, sp_ref, mu_ref, w0_ref, w2_ref, a0_ref, a2_ref, g2_ref, kk_ref, ka_ref, rk_ref, r2_ref,
                      a_out, c_out, w_out, b_out, k_out, v_out, vkr_out, bon_out, g_out, prev_ref, *, tt):
    ti = pl.program_id(1)

    @pl.when(ti == 0)
    def _():
        prev_ref[...] = sp_ref[0]

    p = p_ref[0]
    prev_row = prev_ref[...]
    if tt > 1:
        rolled = pltpu.roll(p, 1, axis=0)
        row = lax.broadcasted_iota(I32, p.shape, 0)
        p_prev = jnp.where(row == 0, prev_row, rolled)
    else:
        p_prev = prev_row
    prev_ref[...] = p[tt - 1:tt, :]
    pm = p + mu_ref[...] * (p_prev - p)
    r = pm[:, 0:RW_W]
    kx = pm[:, RW_W:2 * RW_W]
    vx = pm[:, 2 * RW_W:3 * RW_W]
    lora = pm[:, 3 * RW_W:3 * RW_W + RW_LORA_W]
    gl = pm[:, 3 * RW_W + RW_LORA_W:RW_PROJ]
    dw = jnp.dot(jnp.tanh(lora).astype(BF16), w2_ref[...], preferred_element_type=F32)
    w = -jax.nn.softplus(-(w0_ref[...] + dw)) - 0.5
    decay = jnp.exp(-jnp.exp(w))
    a = jax.nn.sigmoid(a0_ref[...] + jnp.dot(lora.astype(BF16), a2_ref[...], preferred_element_type=F32))
    g = jnp.dot(jax.nn.sigmoid(gl).astype(BF16), g2_ref[...], preferred_element_type=F32)
    kk = kx * kk_ref[...]
    kkn = kk / jnp.maximum(jnp.sqrt(_segsum(kk * kk, r2_ref)), 1e-12)
    k = kx * (1.0 + (a - 1.0) * ka_ref[...])
    av = -kkn
    bv = kkn * a
    br = _segsum(bv * r, r2_ref)
    kr = _segsum(k * r, r2_ref)
    a_out[0] = av
    c_out[0] = decay * r + av * br
    w_out[0] = decay
    b_out[0] = bv
    k_out[0] = k
    v_out[0] = vx
    vkr_out[0] = vx * kr
    bon_out[0] = _segsum(r * k * rk_ref[...], r2_ref) * vx
    g_out[0] = g


def _value_diag():
    row = lax.broadcasted_iota(I32, (RW_N, RW_W), 0)
    lane = lax.broadcasted_iota(I32, (RW_N, RW_W), 1)
    return (lane & (RW_N - 1)) == row


def _scan_step(seq_refs, s_ref, t, e_j, zacc, rmat, diag, *, nb, v_terms):
    a_ref, c_ref, w_ref, b_ref, k_ref, v_ref = seq_refs
    width = rmat.shape[0]
    per = (RW_W // width) * RW_N
    row = lambda ref, bb: ref[bb, pl.ds(t, 1), :]
    nterm = 1 + v_terms
    stacked, pz = [], []
    for bb in range(nb):
        s = s_ref[bb]
        v = row(v_ref, bb)
        v_hi = jnp.where(diag, v, 0.0).astype(BF16)
        terms = [(s * row(a_ref, bb)).astype(BF16), v_hi]
        pz.append((s * row(c_ref, bb)).astype(BF16))
        if v_terms == 2:
            terms.append(jnp.where(diag, v - v_hi.astype(F32), 0.0).astype(BF16))
        stacked += [_stack_lane_blocks(x, width) for x in terms]
    red = jnp.dot(jnp.concatenate(stacked, axis=0), rmat, preferred_element_type=F32)
    for bb in range(nb):
        parts = [_unstack_lane_blocks(red[(nterm * bb + i) * per:(nterm * bb + i + 1) * per], RW_N)
                 for i in range(nterm)]
        vb = parts[1] if v_terms == 1 else parts[1] + parts[2]
        s_ref[bb] = s_ref[bb] * row(w_ref, bb) + parts[0] * row(b_ref, bb) + vb * row(k_ref, bb)
    return zacc + jnp.dot(jnp.concatenate(pz, axis=0), e_j, preferred_element_type=F32)


def _rwkv_scan_kernel(*refs, nb, tt, nst, v_terms):
    seq_refs = refs[:6]
    s0_ref, r_ref, e_ref, y_ref, so_ref, s_ref = refs[6:]
    ti = pl.program_id(1)

    @pl.when(ti == 0)
    def _():
        s_ref[...] = s0_ref[...]

    diag = _value_diag()
    rmat = r_ref[...]

    def group(g, carry):
        zacc = jnp.zeros((nb * RW_N, LANES), F32)
        for j in range(nst):
            zacc = _scan_step(seq_refs, s_ref, g * nst + j, e_ref[j], zacc, rmat, diag, nb=nb, v_terms=v_terms)
        for bb in range(nb):
            y_ref[bb, g] = zacc[bb * RW_N:(bb + 1) * RW_N]
        return carry

    lax.fori_loop(0, tt // nst, group, 0)

    @pl.when(ti == pl.num_programs(1) - 1)
    def _():
        so_ref[...] = s_ref[...]


def _rwkv_post_kernel(y_ref, vkr_ref, bon_ref, g_ref, lw_ref, lb_ref, r2_ref, o_ref):
    y = y_ref[...] + vkr_ref[...]
    mu = _segsum(y, r2_ref) * (1.0 / RW_N)
    d = y - mu
    var = _segsum(d * d, r2_ref) * (1.0 / RW_N)
    yn = d * lax.rsqrt(var + RW_LN_EPS) * lw_ref[...] + lb_ref[...]
    o_ref[...] = ((yn + bon_ref[...]) * g_ref[...]).astype(BF16)


def _segment_ones(width):
    return jnp.kron(jnp.eye(width // RW_N, dtype=F32), jnp.ones((RW_N, RW_N), F32)).astype(BF16)


def _segsum_operand():
    seg = _segment_ones(LANES)
    return jnp.concatenate([seg, seg], axis=0)


def rwkv_prep(prw, shift_prev, prm, batch, seq):
    tt = min(seq, 128)
    r2 = _segsum_operand()
    zeros64 = jnp.zeros((RW_N, RW_W), F32)
    w2p = jnp.concatenate([prm["rw_w2"], zeros64], axis=0).astype(BF16)
    a2p = jnp.concatenate([zeros64, prm["rw_a2"]], axis=0).astype(BF16)
    row = lambda a: a.reshape(1, -1)
    blk_in = pl.BlockSpec((1, tt, RW_PROJ), lambda b, i: (b, i, 0))
    blk_out = pl.BlockSpec((1, tt, RW_W), lambda b, i: (b, i, 0))
    c2 = lambda shape: pl.BlockSpec(shape, lambda b, i: (0,) * len(shape))
    return pl.pallas_call(
        functools.partial(_rwkv_prep_kernel, tt=tt),
        grid=(batch, seq // tt),
        in_specs=[blk_in, pl.BlockSpec((1, 1, RW_PROJ), lambda b, i: (b, 0, 0)), c2((1, RW_PROJ)),
                  c2((1, RW_W)), c2((RW_LORA_W, RW_W)), c2((1, RW_W)), c2((RW_LORA_W, RW_W)),
                  c2((RW_GATE_W, RW_W)), c2((1, RW_W)), c2((1, RW_W)), c2((1, RW_W)), c2((2 * LANES, LANES))],
        out_specs=[blk_out] * 9,
        out_shape=[jax.ShapeDtypeStruct((batch, seq, RW_W), F32)] * 9,
        scratch_shapes=[pltpu.VMEM((1, RW_PROJ), F32)],
        compiler_params=_cparams(("parallel", "arbitrary"), 56),
        name="rwkv_prep",
    )(prw.reshape(batch, seq, RW_PROJ), shift_prev.reshape(batch, 1, RW_PROJ), row(prm["rw_mu"]),
      row(prm["rw_w0"]), w2p, row(prm["rw_a0"]), a2p, prm["rw_g2"].astype(BF16), row(prm["rw_k_k"]),
      row(prm["rw_k_a"]), row(prm["rw_r_k"]), r2)


SCAN_NB = 4


def _scan_group(seq):
    return min(seq, LANES // RW_HEADS)


def _scan_operands(wkv0, batch, seq):
    nst = _scan_group(seq)
    head = jnp.arange(RW_W, dtype=I32) // RW_N
    ecomp = (head[None, :, None] + RW_HEADS * jnp.arange(nst, dtype=I32)[:, None, None]
             == jnp.arange(LANES, dtype=I32)[None, None, :]).astype(BF16)
    s0 = wkv0.astype(F32).transpose(0, 2, 1, 3).reshape(batch, RW_N, RW_W)
    return s0, _segment_ones(MXU_DIM), ecomp


def _scan_out_shapes(batch, seq):
    return [jax.ShapeDtypeStruct((batch, seq // _scan_group(seq), RW_N, LANES), F32),
            jax.ShapeDtypeStruct((batch, RW_N, RW_W), F32)]


def rwkv_scan(seq_arrays, wkv0, batch, seq):
    nst = _scan_group(seq)
    ts = min(seq, 64)
    nb = SCAN_NB
    v_terms = 2 if seq == 1 else 1
    c2 = lambda shape: pl.BlockSpec(shape, lambda b, i: (0,) * len(shape))
    seq_blk = pl.BlockSpec((nb, ts, RW_W), lambda b, i: (b, i, 0))
    grp_blk = pl.BlockSpec((nb, ts // nst, RW_N, LANES), lambda b, i: (b, i, 0, 0))
    st_blk = pl.BlockSpec((nb, RW_N, RW_W), lambda b, i: (b, 0, 0))
    return pl.pallas_call(
        functools.partial(_rwkv_scan_kernel, nb=nb, tt=ts, nst=nst, v_terms=v_terms),
        grid=(batch // nb, seq // ts),
        in_specs=[seq_blk] * 6 + [st_blk, c2((MXU_DIM, MXU_DIM)), c2((nst, RW_W, LANES))],
        out_specs=[grp_blk, st_blk],
        out_shape=_scan_out_shapes(batch, seq),
        scratch_shapes=[pltpu.VMEM((nb, RW_N, RW_W), F32)],
        compiler_params=_cparams(("parallel", "arbitrary"), 48),
        name="rwkv_scan",
    )(*seq_arrays, *_scan_operands(wkv0, batch, seq))


def rwkv_finish(yc, s_fin, vkr, bon, gv, prm, batch, seq):
    m = batch * seq
    nst = _scan_group(seq)
    row = lambda a: a.reshape(1, -1)
    yz = yc.reshape(batch, seq // nst, RW_N, LANES // RW_HEADS, RW_HEADS)[:, :, :, :nst, :]
    yz = yz.transpose(0, 1, 3, 4, 2).reshape(m, RW_W)
    wkv_new = s_fin.reshape(batch, RW_N, RW_HEADS, RW_N).transpose(0, 2, 1, 3)

    tm = min(m, ROW_TILE)
    rows = pl.BlockSpec((tm, RW_W), lambda i: (i, 0))
    y = pl.pallas_call(
        _rwkv_post_kernel,
        grid=(m // tm,),
        in_specs=[rows] * 4 + [_full((1, RW_W)), _full((1, RW_W)), _full((2 * LANES, LANES))],
        out_specs=rows,
        out_shape=jax.ShapeDtypeStruct((m, RW_W), BF16),
        compiler_params=_cparams(("parallel",), 48),
        name="rwkv_post",
    )(yz, vkr.reshape(m, RW_W), bon.reshape(m, RW_W), gv.reshape(m, RW_W), row(prm["rw_ln_w"]),
      row(prm["rw_ln_b"]), _segsum_operand())
    return y, wkv_new


def rwkv_mix(prw, shift_prev, wkv0, prm, batch, seq):
    *seq_arrays, vkr, bon, gv = rwkv_prep(prw, shift_prev, prm, batch, seq)
    yc, s_fin = rwkv_scan(seq_arrays[:6], wkv0, batch, seq)
    return rwkv_finish(yc, s_fin, vkr, bon, gv, prm, batch, seq)


def _merge_kernel(ya_ref, yb_ref, ga_ref, gb_ref, ba_ref, bb_ref, wa_ref, wb_ref, o_ref):
    pa = jnp.dot(ya_ref[...].astype(BF16), wa_ref[...], preferred_element_type=F32)
    pb = jnp.dot(yb_ref[...], wb_ref[...], preferred_element_type=F32)
    ga = jax.nn.sigmoid(ga_ref[...] + ba_ref[...])
    gb = jax.nn.sigmoid(gb_ref[...] + bb_ref[...])
    o_ref[...] = (ga * pa + gb * pb).astype(BF16)


def merge_branches(y_att, y_rw, g_lin, b_gate, wa, wb):
    m = y_att.shape[0]
    tm = min(m, ROW_TILE)
    d = D_MODEL
    return pl.pallas_call(
        _merge_kernel,
        grid=(m // tm,),
        in_specs=[pl.BlockSpec((tm, ATT_W), lambda i: (i, 0)), pl.BlockSpec((tm, RW_W), lambda i: (i, 0)),
                  pl.BlockSpec((tm, d), lambda i: (i, 0)), pl.BlockSpec((tm, d), lambda i: (i, 1)),
                  pl.BlockSpec((1, d), lambda i: (0, 0)), pl.BlockSpec((1, d), lambda i: (0, 1)),
                  _full((ATT_W, d)), _full((RW_W, d))],
        out_specs=pl.BlockSpec((tm, d), lambda i: (i, 0)),
        out_shape=jax.ShapeDtypeStruct((m, d), BF16),
        compiler_params=_cparams(("parallel",), 48),
        name="merge_branches",
    )(y_att, y_rw, g_lin, g_lin, b_gate.reshape(1, 2 * d), b_gate.reshape(1, 2 * d), wa, wb)


SLAB = D_MODEL // LANES


def _store_token_slabs(ref, x):
    n = x.shape[0]
    for c in range(SLAB):
        ref[pl.ds(c, n, stride=SLAB), :] = x[:, c * LANES:(c + 1) * LANES]


def _load_token_slabs(ref, n):
    return jnp.concatenate([ref[pl.ds(c, n, stride=SLAB), :] for c in range(SLAB)], axis=1)


def _outproj_router_kernel(mg_ref, wo_ref, x_ref, nw_ref, wr_ref, br_ref, h_ref, hn_ref, eid_ref, ew_ref):
    h = x_ref[...] + jnp.dot(mg_ref[...], wo_ref[...], preferred_element_type=F32)
    h_ref[...] = h
    ms = jnp.mean(h * h, axis=-1, keepdims=True)
    hn = h * lax.rsqrt(ms + NORM_EPS) * nw_ref[...]
    _store_token_slabs(hn_ref, hn)
    lg = jnp.dot(hn.astype(BF16), wr_ref[...], preferred_element_type=F32) + br_ref[...]
    lane = lax.broadcasted_iota(I32, lg.shape, 1)
    lane_f = lane.astype(F32)
    big = float(LANES)
    lgm = jnp.where(lane < N_GROUPS, lg, NEG)
    gmax = jnp.max(lgm, axis=-1, keepdims=True)
    gidx = jnp.min(jnp.where(lgm == gmax, lane_f, big), axis=-1, keepdims=True)
    g_w = 1.0 / jnp.sum(jnp.exp(lgm - gmax), axis=-1, keepdims=True)
    lane_group = ((lane >> 3) - 1).astype(F32)
    in_group = (lane >= N_GROUPS) & (lane < N_GROUPS + N_EXPERTS) & (lane_group == gidx)
    le = jnp.where(in_group, lg, NEG)
    m1 = jnp.max(le, axis=-1, keepdims=True)
    i1 = jnp.min(jnp.where(le == m1, lane_f, big), axis=-1, keepdims=True)
    le2 = jnp.where(lane_f == i1, NEG, le)
    m2 = jnp.max(le2, axis=-1, keepdims=True)
    i2 = jnp.min(jnp.where(le2 == m2, lane_f, big), axis=-1, keepdims=True)
    e2 = jnp.exp(m2 - m1)
    w1 = g_w / (1.0 + e2)
    w2 = g_w * e2 / (1.0 + e2)
    col = lax.broadcasted_iota(I32, eid_ref.shape, 1)
    eid_ref[...] = jnp.where(col == 0, i1, i2).astype(I32) - N_GROUPS
    ew_ref[...] = jnp.where(col == 0, w1, w2)


def outproj_router(merged, w_out, x, norm_w, w_router, b_router):
    m, d = x.shape
    tm = min(m, ROW_TILE)
    rows = lambda w: pl.BlockSpec((tm, w), lambda i: (i, 0))
    return pl.pallas_call(
        _outproj_router_kernel,
        grid=(m // tm,),
        in_specs=[rows(d), _full((d, d)), rows(d), _full((1, d)), _full((d, LANES)), _full((1, LANES))],
        out_specs=[rows(d), pl.BlockSpec((tm * SLAB, LANES), lambda i: (i, 0)), rows(2), rows(2)],
        out_shape=[jax.ShapeDtypeStruct((m, d), F32), jax.ShapeDtypeStruct((m * SLAB, LANES), F32),
                   jax.ShapeDtypeStruct((m, 2), I32), jax.ShapeDtypeStruct((m, 2), F32)],
        compiler_params=_cparams(("parallel",), 48),
        name="outproj_router",
    )(merged, w_out, x, norm_w.reshape(1, d), w_router, b_router)


def _moe_kernel(starts_ref, counts_ref, pcounts_ref, src_ref, hp_ref, hs_ref, wg_ref, wu_ref, wd_ref,
                yp_ref, ys_ref, wgb, wub, wdb, xbuf, ybuf, sem_in, sem_out, *, n_prompt, n_sample):
    e = pl.program_id(0)

    @pl.when(e == 0)
    def _():
        xbuf[...] = jnp.zeros(xbuf.shape, F32)

    n = counts_ref[e]
    n_p = pcounts_ref[e]
    start = starts_ref[e]
    nblk = (n + MOE_BLK - 1) // MOE_BLK

    def slab(i):
        return pl.ds(pl.multiple_of(i * SLAB, SLAB), SLAB)

    def row_in(src_hbm, tok, slot, r):
        return pltpu.make_async_copy(src_hbm.at[slab(tok)], xbuf.at[slot, slab(r)], sem_in.at[slot])

    def row_out(dst_hbm, dst, r):
        return pltpu.make_async_copy(ybuf.at[slab(r)], dst_hbm.at[slab(dst)], sem_out)

    def for_rows(lo, hi, body):
        groups = (hi - lo) // MOE_UNROLL

        def group(i, c):
            for u in range(MOE_UNROLL):
                body(lo + i * MOE_UNROLL + u)
            return c

        lax.fori_loop(0, groups, group, 0)
        lax.fori_loop(lo + groups * MOE_UNROLL, hi, lambda r, c: (body(r), c)[1], 0)

    def block_rows(bi):
        nv = jnp.minimum(MOE_BLK, n - bi * MOE_BLK)
        nvp = jnp.clip(n_p - bi * MOE_BLK, 0, nv)
        return start + bi * MOE_BLK, nv, nvp

    def gather_start(bi, slot):
        r0, nv, nvp = block_rows(bi)
        for_rows(0, nvp, lambda r: row_in(hp_ref, src_ref[r0 + r] >> 1, slot, r).start())
        for_rows(nvp, nv, lambda r: row_in(hs_ref, (src_ref[r0 + r] >> 1) - n_prompt, slot, r).start())

    def gather_wait(bi, slot):
        _, nv, _ = block_rows(bi)
        for_rows(0, nv, lambda r: row_in(hp_ref, 0, slot, r).wait())

    def scatter_start(bi):
        r0, nv, nvp = block_rows(bi)

        def to_prompt(r):
            src = src_ref[r0 + r]
            row_out(yp_ref, (src & 1) * n_prompt + (src >> 1), r).start()

        def to_sample(r):
            src = src_ref[r0 + r]
            row_out(ys_ref, (src & 1) * n_sample + (src >> 1) - n_prompt, r).start()

        for_rows(0, nvp, to_prompt)
        for_rows(nvp, nv, to_sample)

    def scatter_wait(bi):
        _, nv, _ = block_rows(bi)
        for_rows(0, nv, lambda r: row_out(yp_ref, 0, r).wait())

    @pl.when(n > 0)
    def _():
        gather_start(0, 0)
        wgb[...] = wg_ref[0].astype(BF16)
        wub[...] = wu_ref[0].astype(BF16)
        wdb[...] = wd_ref[0].astype(BF16)

        def block(bi, carry):
            slot = bi & 1
            gather_wait(bi, slot)

            @pl.when(bi + 1 < nblk)
            def _():
                gather_start(bi + 1, 1 - slot)

            x = _load_token_slabs(xbuf.at[slot], MOE_BLK).astype(BF16)
            hg = jnp.dot(x, wgb[...], preferred_element_type=F32)
            hu = jnp.dot(x, wub[...], preferred_element_type=F32)
            hd = (hg * jax.nn.sigmoid(hg) * hu).astype(BF16)
            y = jnp.dot(hd, wdb[...], preferred_element_type=F32)

            @pl.when(bi > 0)
            def _():
                scatter_wait(bi - 1)

            _store_token_slabs(ybuf, y)
            scatter_start(bi)
            return carry

        lax.fori_loop(0, nblk, block, 0)
        scatter_wait(nblk - 1)


def moe_experts(hn_p, hn_s, eid, w_gate, w_up, w_down):
    d = D_MODEL
    n_prompt = hn_p.shape[0] // SLAB
    n_sample = hn_s.shape[0] // SLAB
    n_rows = eid.shape[0] * 2
    flat_e = eid.reshape(n_rows)
    onehot = (flat_e[:, None] == jnp.arange(N_EXPERTS, dtype=I32)[None, :]).astype(I32)
    counts = jnp.sum(onehot, axis=0)
    pcounts = jnp.sum(onehot[:2 * n_prompt], axis=0)
    starts = jnp.cumsum(counts) - counts
    rank = jnp.sum(onehot * (jnp.cumsum(onehot, axis=0) - 1), axis=1)
    pos = starts[flat_e] + rank
    src = jnp.zeros((n_rows,), I32).at[pos].set(jnp.arange(n_rows, dtype=I32), unique_indices=True,
                                                mode="promise_in_bounds")
    any_spec = pl.BlockSpec(memory_space=pl.ANY)
    grid_spec = pltpu.PrefetchScalarGridSpec(
        num_scalar_prefetch=4,
        grid=(N_EXPERTS,),
        in_specs=[any_spec, any_spec,
                  pl.BlockSpec((1, d, D_FF), lambda e, *_: (e, 0, 0)),
                  pl.BlockSpec((1, d, D_FF), lambda e, *_: (e, 0, 0)),
                  pl.BlockSpec((1, D_FF, d), lambda e, *_: (e, 0, 0))],
        out_specs=[any_spec, any_spec],
        scratch_shapes=[pltpu.VMEM((d, D_FF), BF16), pltpu.VMEM((d, D_FF), BF16), pltpu.VMEM((D_FF, d), BF16),
                        pltpu.VMEM((2, MOE_BLK * SLAB, LANES), F32), pltpu.VMEM((MOE_BLK * SLAB, LANES), F32),
                        pltpu.SemaphoreType.DMA((2,)), pltpu.SemaphoreType.DMA(())],
    )
    return pl.pallas_call(
        functools.partial(_moe_kernel, n_prompt=n_prompt, n_sample=n_sample),
        grid_spec=grid_spec,
        out_shape=[jax.ShapeDtypeStruct((2 * n_prompt * SLAB, LANES), F32),
                   jax.ShapeDtypeStruct((2 * n_sample * SLAB, LANES), F32)],
        compiler_params=_cparams(("arbitrary",), 56),
        name="moe_experts",
    )(starts, counts, pcounts, src, hn_p, hn_s, w_gate, w_up, w_down)


def _final_kernel(h_ref, y0_ref, y1_ref, ew_ref, nw_ref, o_ref):
    ew = ew_ref[...]
    tm = h_ref.shape[0]
    x = (h_ref[...] + ew[:, 0:1] * _load_token_slabs(y0_ref, tm) + ew[:, 1:2] * _load_token_slabs(y1_ref, tm))
    ms = jnp.mean(x * x, axis=-1, keepdims=True)
    o_ref[...] = x * lax.rsqrt(ms + NORM_EPS) * nw_ref[...]


def combine_final(h, y_rows, ew, norm_w):
    m, d = h.shape
    tm = min(m, ROW_TILE)
    o1 = m // tm
    return pl.pallas_call(
        _final_kernel,
        grid=(m // tm,),
        in_specs=[pl.BlockSpec((tm, d), lambda i: (i, 0)),
                  pl.BlockSpec((tm * SLAB, LANES), lambda i: (i, 0)),
                  pl.BlockSpec((tm * SLAB, LANES), lambda i: (i + o1, 0)),
                  pl.BlockSpec((tm, 2), lambda i: (i, 0)), _full((1, d))],
        out_specs=pl.BlockSpec((tm, d), lambda i: (i, 0)),
        out_shape=jax.ShapeDtypeStruct((m, d), F32),
        compiler_params=_cparams(("parallel",), 48),
        name="combine_final",
    )(h, y_rows, y_rows, ew, norm_w.reshape(1, d))


_RW_COL0 = 3 * ATT_W
_GATE_COL0 = _RW_COL0 + RW_PROJ


def _in_proj(x2, prm, q_dtype):
    xn = rmsnorm_bf16(x2, prm["norm_mix_w"])
    w_in = prm["w_in"]
    q = matmul_cols(xn, w_in, 0, ATT_W, q_dtype)
    k = matmul_cols(xn, w_in, ATT_W, ATT_W, F32)
    v = matmul_cols(xn, w_in, 2 * ATT_W, ATT_W, F32)
    prw = matmul_cols(xn, w_in, _RW_COL0, RW_PROJ, F32)
    g_lin = matmul_cols(xn, w_in, _GATE_COL0, 2 * D_MODEL, F32)
    return q, k, v, prw, g_lin


def _mixer_tail(x2, y_att, y_rw, g_lin, prm):
    merged = merge_branches(y_att, y_rw, g_lin, prm["b_gate"], prm["wa_bf16"], prm["wb_bf16"])
    return outproj_router(merged, prm["wo_bf16"], x2, prm["norm_ffn_w"], prm["w_router"], prm["b_router"])


def kernel(x_prompt, x_sample, cache_k, cache_v, state_wkv, state_shift, page_table, norm_mix_w, w_in, b_gate, lambda_q1, lambda_k1, lambda_q2, lambda_k2, subln_w, rel_bias, rw_mu, rw_w0, rw_w2, rw_a0, rw_a2, rw_g2, rw_k_k, rw_k_a, rw_r_k, rw_ln_w, rw_ln_b, w_branch_a, w_branch_b, w_out, norm_ffn_w, w_router_g, b_router_g, w_router_e, b_router_e, w_gate_e, w_up_e, w_down_e, norm_final_w):
    assert w_in.shape[0] == 1, "single-layer trunk"
    bp, sp, d = x_prompt.shape
    bs, ss, _ = x_sample.shape
    assert ss == 1 and d == D_MODEL
    n_pool = cache_k.shape[1]
    pad_lanes = LANES - N_GROUPS - N_EXPERTS
    prm = {
        "norm_mix_w": norm_mix_w[0], "w_in": w_in[0], "b_gate": b_gate[0],
        "rw_mu": rw_mu[0], "rw_w0": rw_w0[0], "rw_w2": rw_w2[0], "rw_a0": rw_a0[0], "rw_a2": rw_a2[0],
        "rw_g2": rw_g2[0], "rw_k_k": rw_k_k[0], "rw_k_a": rw_k_a[0], "rw_r_k": rw_r_k[0].reshape(RW_W),
        "rw_ln_w": rw_ln_w[0], "rw_ln_b": rw_ln_b[0],
        "wa_bf16": w_branch_a[0].astype(BF16), "wb_bf16": w_branch_b[0].astype(BF16),
        "wo_bf16": w_out[0].astype(BF16), "norm_ffn_w": norm_ffn_w[0],
        "w_router": jnp.concatenate([w_router_g[0], w_router_e[0], jnp.zeros((d, pad_lanes), F32)],
                                    axis=1).astype(BF16),
        "b_router": jnp.concatenate([b_router_g[0], b_router_e[0], jnp.zeros((pad_lanes,), F32)]).reshape(1, LANES),
    }
    lamv = jnp.stack([lambda_q1[0], lambda_k1[0], lambda_q2[0], lambda_k2[0]]).astype(F32)
    sub_w = subln_w[0]

    mp = bp * sp
    xp2 = x_prompt.reshape(mp, d)
    q_p, k_p, v_p, prw_p, g_p = _in_proj(xp2, prm, BF16)
    att_p = attn_prompt(q_p, k_p, v_p, lamv, rel_bias, sub_w, bp, sp)
    rw_p, wkv_p = rwkv_mix(prw_p, jnp.zeros((bp, RW_PROJ), F32), jnp.zeros((bp, RW_HEADS, RW_N, RW_N), F32),
                           prm, bp, sp)
    h_p, hn_p, eid_p, ew_p = _mixer_tail(xp2, att_p, rw_p, g_p, prm)

    xs2 = x_sample.reshape(bs, d)
    q_s, k_s, v_s, prw_s, g_s = _in_proj(xs2, prm, F32)
    heads = lambda a: a.reshape(bs, ATT_HEADS, HEAD_W)
    att_s = attn_decode(heads(q_s), heads(k_s), heads(v_s),
                        cache_k[0].reshape(n_pool, PAGE * ATT_HEADS, HEAD_W),
                        cache_v[0].reshape(n_pool, PAGE * ATT_HEADS, HEAD_W),
                        page_table, lamv, rel_bias, sub_w).reshape(bs, ATT_W)
    rw_s, wkv_s = rwkv_mix(prw_s, state_shift[0], state_wkv[0], prm, bs, 1)
    h_s, hn_s, eid_s, ew_s = _mixer_tail(xs2, att_s, rw_s, g_s, prm)

    yr_p, yr_s = moe_experts(hn_p, hn_s, jnp.concatenate([eid_p, eid_s], axis=0),
                             w_gate_e[0], w_up_e[0], w_down_e[0])
    y_p = combine_final(h_p, yr_p, ew_p, norm_final_w)
    y_s = combine_final(h_s, yr_s, ew_s, norm_final_w)

    return (y_p.reshape(bp, sp, d), y_s.reshape(bs, 1, d),
            k_p.reshape(1, bp, sp, ATT_HEADS, HEAD_W), v_p.reshape(1, bp, sp, ATT_HEADS, HEAD_W),
            wkv_p.astype(state_wkv.dtype)[None], prw_p.reshape(bp, sp, RW_PROJ)[None, :, -1],
            k_s.reshape(1, bs, 1, ATT_HEADS, HEAD_W), v_s.reshape(1, bs, 1, ATT_HEADS, HEAD_W),
            wkv_s.astype(state_wkv.dtype)[None], prw_s.reshape(1, bs, RW_PROJ))
```
